```python
import math
import jax
import jax.numpy as jnp
from jax import lax
import numpy as np

D_MODEL = 2048
BATCH = 1
SEQ = 16384
DEPTH = 1

CHUNK = 64
Q_BLOCK = 128
MEM_LEN = 256
EPS = 1e-6
MIX_WIDTH = D_MODEL
ATTN_WIDTH = MIX_WIDTH // 2
DA_HEAD_DIM = 64
DA_HEADS = ATTN_WIDTH // (2 * DA_HEAD_DIM)
RNN_WIDTH = MIX_WIDTH - ATTN_WIDTH
RNN_BLOCKS = 8
RNN_BLOCK = RNN_WIDTH // RNN_BLOCKS
CONV_WIDTH = 4
RG_C = 8.0
X_HEADS = 4
X_HEAD_DIM = 128
X_WIDTH = X_HEADS * X_HEAD_DIM
D_FF = -(-8 * D_MODEL // (3 * 256)) * 256
IN_WIDTH = 3 * ATTN_WIDTH + 2 * RNN_WIDTH
NUM_BUCKETS = 32
MAX_DISTANCE = 1024

kernel_name = 'hybrid_diffattn_rglru_memxattn_layer'


def lambda_init(layer):
    return 0.8 - 0.6 * math.exp(-0.3 * layer)


def rmsnorm(x, gain):
    x32 = x.astype(jnp.float32)
    y = x32 * lax.rsqrt(jnp.mean(x32 * x32, axis=-1, keepdims=True) + EPS)
    return (y * gain.astype(jnp.float32)).astype(x.dtype)


def t5_bucket(rel):
    half = NUM_BUCKETS // 2
    max_exact = half // 2
    n = jnp.abs(rel)
    large = max_exact + (jnp.log(jnp.maximum(n, 1).astype(jnp.float32) / max_exact)
                         / math.log(MAX_DISTANCE / max_exact) * (half - max_exact)).astype(jnp.int32)
    large = jnp.minimum(large, half - 1)
    return jnp.where(rel > 0, half, 0) + jnp.where(n < max_exact, n, large)


def diff_attention(q, k, v, rel_bias, lam):
    b, s = q.shape[0], q.shape[1]
    n_blocks = s // Q_BLOCK
    q_blocks = q.reshape(b, n_blocks, Q_BLOCK, DA_HEADS, 2, DA_HEAD_DIM).swapaxes(0, 1)
    k1, k2 = k[..., 0, :], k[..., 1, :]
    k_pos = jnp.arange(s)
    scale = DA_HEAD_DIM ** -0.5

    def block(args):
        i, qb = args
        q_pos = i * Q_BLOCK + jnp.arange(Q_BLOCK)
        bias = rel_bias[t5_bucket(k_pos[None, :] - q_pos[:, None])].astype(jnp.float32).transpose(2, 0, 1)
        allowed = (k_pos[None, :] // CHUNK) <= (q_pos[:, None] // CHUNK)

        def probs(qh, kh):
            logits = jnp.einsum('bqhd,bkhd->bhqk', qh, kh).astype(jnp.float32) * scale + bias
            return jax.nn.softmax(jnp.where(allowed, logits, -jnp.inf), axis=-1)

        attn = probs(qb[..., 0, :], k1) - lam * probs(qb[..., 1, :], k2)
        return jnp.einsum('bhqk,bkhe->bqhe', attn.astype(v.dtype), v)

    out = lax.map(block, (jnp.arange(n_blocks), q_blocks))
    return out.swapaxes(0, 1).reshape(b, s, DA_HEADS, 2 * DA_HEAD_DIM)


def linear_scan_combine(left, right):
    a_l, b_l = left
    a_r, b_r = right
    return a_l * a_r, a_r * b_l + b_r


def rg_lru_branch(xr, conv_w, conv_b, ga_w, ga_b, gx_w, gx_b, rg_lambda):
    b, s, _ = xr.shape
    xc = lax.conv_general_dilated(xr, conv_w[:, None, :].astype(xr.dtype), window_strides=(1,),
                                  padding=[(CONV_WIDTH - 1, 0)], dimension_numbers=('NWC', 'WIO', 'NWC'),
                                  feature_group_count=RNN_WIDTH) + conv_b
    xb = xc.reshape(b, s, RNN_BLOCKS, RNN_BLOCK)
    gate_a = jax.nn.sigmoid(jnp.einsum('bsgi,gij->bsgj', xb, ga_w).reshape(b, s, RNN_WIDTH) + ga_b)
    gate_x = jax.nn.sigmoid(jnp.einsum('bsgi,gij->bsgj', xb, gx_w).reshape(b, s, RNN_WIDTH) + gx_b)
    log_a = -RG_C * gate_a.astype(jnp.float32) * jax.nn.softplus(-rg_lambda.astype(jnp.float32))
    a = jnp.exp(log_a)
    u = (xc * gate_x).astype(jnp.float32) * jnp.sqrt(-jnp.expm1(2.0 * log_a))
    _, h = lax.associative_scan(linear_scan_combine, (a, u), axis=1)
    return h.astype(xr.dtype)


def memory_cross_attention(h, mem, mem_gain, wq, wk, wv, q_gain, k_gain, wo):
    b, s, _ = h.shape
    m = rmsnorm(mem, mem_gain)
    q = rmsnorm((h @ wq).reshape(b, s, X_HEADS, X_HEAD_DIM), q_gain)
    k = rmsnorm((m @ wk).reshape(b, -1, X_HEADS, X_HEAD_DIM), k_gain)
    v = (m @ wv).reshape(b, -1, X_HEADS, X_HEAD_DIM)
    logits = jnp.einsum('bqhd,bkhd->bhqk', q, k).astype(jnp.float32) * (X_HEAD_DIM ** -0.5)
    p = jax.nn.softmax(logits, axis=-1)
    o = jnp.einsum('bhqk,bkhd->bqhd', p.astype(v.dtype), v).reshape(b, s, X_WIDTH)
    return o @ wo


def setup_inputs(seed: int = 0) -> dict:
    key = jax.random.key(seed)
    ks = jax.random.split(key, 32)
    L = DEPTH

    def nrm(k, shape, scale):
        return jax.random.normal(k, shape, jnp.float32) * scale

    def gain(k, shape):
        return 1.0 + 0.02 * jax.random.normal(k, shape, jnp.float32)

    u = jax.random.uniform(ks[18], (L, RNN_WIDTH), jnp.float32, minval=0.9, maxval=0.999)
    a0 = u ** (1.0 / RG_C)
    rg_lambda = jnp.log(a0) - jnp.log1p(-a0)
    return {
        'x': nrm(ks[0], (BATCH, SEQ, D_MODEL), 1.0),
        'mem': nrm(ks[1], (BATCH, MEM_LEN, D_MODEL), 1.0),
        'rel_bias': nrm(ks[2], (NUM_BUCKETS, DA_HEADS), 0.2),
        'attn_norm': gain(ks[3], (L, D_MODEL)),
        'w_in': nrm(ks[4], (L, D_MODEL, IN_WIDTH), D_MODEL ** -0.5),
        'da_q_norm': gain(ks[5], (L, DA_HEAD_DIM)),
        'da_k_norm': gain(ks[6], (L, DA_HEAD_DIM)),
        'da_lambda_q1': nrm(ks[7], (L, DA_HEAD_DIM), 0.1),
        'da_lambda_k1': nrm(ks[8], (L, DA_HEAD_DIM), 0.1),
        'da_lambda_q2': nrm(ks[9], (L, DA_HEAD_DIM), 0.1),
        'da_lambda_k2': nrm(ks[10], (L, DA_HEAD_DIM), 0.1),
        'da_subln': gain(ks[11], (L, 2 * DA_HEAD_DIM)),
        'conv_w': nrm(ks[12], (L, CONV_WIDTH, RNN_WIDTH), CONV_WIDTH ** -0.5),
        'conv_b': nrm(ks[13], (L, RNN_WIDTH), 0.01),
        'gate_a_w': nrm(ks[14], (L, RNN_BLOCKS, RNN_BLOCK, RNN_BLOCK), RNN_BLOCK ** -0.5),
        'gate_a_b': nrm(ks[15], (L, RNN_WIDTH), 0.01),
        'gate_x_w': nrm(ks[16], (L, RNN_BLOCKS, RNN_BLOCK, RNN_BLOCK), RNN_BLOCK ** -0.5),
        'gate_x_b': nrm(ks[17], (L, RNN_WIDTH), 0.01),
        'rg_lambda': rg_lambda,
        'rnn_out_norm': gain(ks[19], (L, RNN_WIDTH)),
        'w_out': nrm(ks[20], (L, MIX_WIDTH, D_MODEL), MIX_WIDTH ** -0.5),
        'xattn_norm': gain(ks[21], (L, D_MODEL)),
        'mem_norm': gain(ks[22], (L, D_MODEL)),
        'xq_w': nrm(ks[23], (L, D_MODEL, X_WIDTH), D_MODEL ** -0.5),
        'xk_w': nrm(ks[24], (L, D_MODEL, X_WIDTH), D_MODEL ** -0.5),
        'xv_w': nrm(ks[25], (L, D_MODEL, X_WIDTH), D_MODEL ** -0.5),
        'xq_norm': gain(ks[26], (L, X_HEAD_DIM)),
        'xk_norm': gain(ks[27], (L, X_HEAD_DIM)),
        'xo_w': nrm(ks[28], (L, X_WIDTH, D_MODEL), X_WIDTH ** -0.5),
        'ffn_norm': gain(ks[29], (L, D_MODEL)),
        'w_gate_up': nrm(ks[30], (L, D_MODEL, 2 * D_FF), D_MODEL ** -0.5),
        'w_down': nrm(ks[31], (L, D_FF, D_MODEL), D_FF ** -0.5),
    }


def reference(x, mem, rel_bias, attn_norm, w_in, da_q_norm, da_k_norm, da_lambda_q1, da_lambda_k1,
              da_lambda_q2, da_lambda_k2, da_subln, conv_w, conv_b, gate_a_w, gate_a_b, gate_x_w, gate_x_b,
              rg_lambda, rnn_out_norm, w_out, xattn_norm, mem_norm, xq_w, xk_w, xv_w, xq_norm, xk_norm,
              xo_w, ffn_norm, w_gate_up, w_down):
    b, s, _ = x.shape
    for l in range(DEPTH):
        lam_init = lambda_init(l)
        h = rmsnorm(x, attn_norm[l])
        proj = h @ w_in[l]
        q, k, v, xr, yr = jnp.split(proj, [ATTN_WIDTH, 2 * ATTN_WIDTH, 3 * ATTN_WIDTH,
                                           3 * ATTN_WIDTH + RNN_WIDTH], axis=-1)
        q = rmsnorm(q.reshape(b, s, DA_HEADS, 2, DA_HEAD_DIM), da_q_norm[l])
        k = rmsnorm(k.reshape(b, s, DA_HEADS, 2, DA_HEAD_DIM), da_k_norm[l])
        v = v.reshape(b, s, DA_HEADS, 2 * DA_HEAD_DIM)
        f32 = jnp.float32
        lam = (jnp.exp(jnp.sum(da_lambda_q1[l].astype(f32) * da_lambda_k1[l].astype(f32)))
               - jnp.exp(jnp.sum(da_lambda_q2[l].astype(f32) * da_lambda_k2[l].astype(f32))) + lam_init)
        o_attn = diff_attention(q, k, v, rel_bias, lam)
        o_attn = (rmsnorm(o_attn, da_subln[l]) * (1.0 - lam_init)).reshape(b, s, ATTN_WIDTH)
        h_rnn = rg_lru_branch(xr, conv_w[l], conv_b[l], gate_a_w[l], gate_a_b[l], gate_x_w[l],
                              gate_x_b[l], rg_lambda[l])
        o_rnn = rmsnorm(h_rnn * jax.nn.gelu(yr), rnn_out_norm[l])
        x = x + jnp.concatenate([o_attn, o_rnn], axis=-1) @ w_out[l]
        x = x + memory_cross_attention(rmsnorm(x, xattn_norm[l]), mem, mem_norm[l], xq_w[l], xk_w[l],
                                       xv_w[l], xq_norm[l], xk_norm[l], xo_w[l])
        g, u = jnp.split(rmsnorm(x, ffn_norm[l]) @ w_gate_up[l], 2, axis=-1)
        x = x + (jax.nn.silu(g) * u) @ w_down[l]
    return x
```

```python
import functools
import math

import jax
import jax.numpy as jnp
from jax import lax
from jax.experimental import pallas as pl
from jax.experimental.pallas import tpu as pltpu

F32 = jnp.float32
BF16 = jnp.bfloat16

D_MODEL = 2048
SEQ = 16384
CHUNK = 64
CHUNK_SHIFT = CHUNK.bit_length() - 1
assert 1 << CHUNK_SHIFT == CHUNK
MEM_LEN = 256
EPS = 1e-6
ATTN_WIDTH = 1024
DA_HEAD_DIM = 64
DA_HEADS = 8
DA_V_DIM = 2 * DA_HEAD_DIM
RNN_WIDTH = 1024
RNN_BLOCKS = 8
RNN_BLOCK = RNN_WIDTH // RNN_BLOCKS
CONV_WIDTH = 4
RG_C = 8.0
X_HEADS = 4
X_HEAD_DIM = 128
X_WIDTH = X_HEADS * X_HEAD_DIM
D_FF = 5632
NUM_BUCKETS = 32
MAX_DISTANCE = 1024
LAM_INIT = 0.8 - 0.6 * math.exp(-0.3 * 0)
LOG2E = 1.4426950408889634
NEG_BIG = -1e30

LANES = 128
SUBLANES = 8
MXU_DIM = 256
VMEM_LIMIT = 56 * 1024 * 1024

TM_PROJ = 512
SEG = 1024
T_ATT = 512
TR_RNN = 256
TF_FFN = 512


def _far_bucket_distance():
    half = NUM_BUCKETS // 2
    max_exact = half // 2
    n = max_exact
    while max_exact + int(math.log(n / max_exact) / math.log(MAX_DISTANCE / max_exact) * (half - max_exact)) < half - 1:
        n += 1
    return n


FAR_DIST = _far_bucket_distance()
ND_ATT = next(d for d in range(1, 64) if (d - 1) * T_ATT + 1 >= FAR_DIST + 64)
FAR_BUCKET = NUM_BUCKETS // 2 - 1


def _rms(x, eps=EPS):
    return x * lax.rsqrt(jnp.mean(x * x, axis=-1, keepdims=True) + eps)


def _inproj_kernel(x_ref, g_ref, w_ref, gsum_ref, qg_ref, kg_ref, outT_ref, outN_ref, h_scr):
    j = pl.program_id(1)

    @pl.when(j == 0)
    def _():
        h_scr[...] = (_rms(x_ref[...]) * g_ref[...]).astype(BF16)

    acc = jnp.dot(h_scr[...], w_ref[...], preferred_element_type=F32)

    def group_norm(a, gain_row):
        sq = (a * a).astype(BF16)
        parts = [jnp.dot(sq[:, c * MXU_DIM:(c + 1) * MXU_DIM], gsum_ref[...], preferred_element_type=F32)
                 for c in range(SEG // MXU_DIM)]
        ms = jnp.concatenate(parts, axis=-1)
        return a * lax.rsqrt(ms + EPS) * gain_row

    @pl.when(j == 0)
    def _():
        outT_ref[...] = group_norm(acc, qg_ref[...]).T.astype(BF16)

    @pl.when(j == 1)
    def _():
        outT_ref[...] = acc.T.astype(BF16)

    @pl.when(j == 2)
    def _():
        outN_ref[...] = group_norm(acc, kg_ref[...]).astype(BF16)

    @pl.when(j >= 3)
    def _():
        outN_ref[...] = acc.astype(BF16)


def _in_proj(x, g, w, gsum, qg, kg):
    s = x.shape[0]
    nseg = w.shape[1] // SEG
    return pl.pallas_call(
        _inproj_kernel,
        grid=(s // TM_PROJ, nseg),
        in_specs=[
            pl.BlockSpec((TM_PROJ, D_MODEL), lambda i, j: (i, 0)),
            pl.BlockSpec((1, D_MODEL), lambda i, j: (0, 0)),
            pl.BlockSpec((D_MODEL, SEG), lambda i, j: (0, j)),
            pl.BlockSpec((MXU_DIM, MXU_DIM), lambda i, j: (0, 0)),
            pl.BlockSpec((1, SEG), lambda i, j: (0, 0)),
            pl.BlockSpec((1, SEG), lambda i, j: (0, 0)),
        ],
        out_specs=[
            pl.BlockSpec((None, SEG, TM_PROJ), lambda i, j: (jnp.minimum(j, 1), 0, i)),
            pl.BlockSpec((None, TM_PROJ, SEG), lambda i, j: (jnp.maximum(j - 2, 0), i, 0)),
        ],
        out_shape=[
            jax.ShapeDtypeStruct((2, SEG, s), BF16),
            jax.ShapeDtypeStruct((3, s, SEG), BF16),
        ],
        scratch_shapes=[pltpu.VMEM((TM_PROJ, D_MODEL), BF16)],
        compiler_params=pltpu.CompilerParams(
            dimension_semantics=("arbitrary", "arbitrary"), vmem_limit_bytes=VMEM_LIMIT),
        name="in_proj",
    )(x, g, w, gsum, qg, kg)


def _bias_kernel(rb_ref, out_ref):
    d = pl.program_id(0)
    h = pl.program_id(1)
    t = T_ATT
    kk = lax.broadcasted_iota(jnp.int32, (t, t), 0)
    qq = lax.broadcasted_iota(jnp.int32, (t, t), 1)
    rel = kk - qq - d * t
    half = NUM_BUCKETS // 2
    max_exact = half // 2
    n = jnp.abs(rel)
    large = max_exact + (jnp.log(jnp.maximum(n, 1).astype(F32) / max_exact)
                         / math.log(MAX_DISTANCE / max_exact) * (half - max_exact)).astype(jnp.int32)
    large = jnp.minimum(large, half - 1)
    bucket = jnp.where(rel > 0, half, 0) + jnp.where(n < max_exact, n, large)
    val = jnp.zeros((t, t), F32)
    for b in range(NUM_BUCKETS):
        val = jnp.where(bucket == b, rb_ref[b, h], val)
    val = (val - rb_ref[FAR_BUCKET, h]) * LOG2E
    allowed = (kk >> CHUNK_SHIFT) <= ((qq + d * t) >> CHUNK_SHIFT)
    out_ref[...] = jnp.where(allowed, val, NEG_BIG)


def _bias_tiles(rel_bias):
    return pl.pallas_call(
        _bias_kernel,
        grid=(ND_ATT + 1, DA_HEADS),
        in_specs=[pl.BlockSpec(memory_space=pltpu.SMEM)],
        out_specs=pl.BlockSpec((None, None, T_ATT, T_ATT), lambda d, h: (d, h, 0, 0)),
        out_shape=jax.ShapeDtypeStruct((ND_ATT + 1, DA_HEADS, T_ATT, T_ATT), F32),
        compiler_params=pltpu.CompilerParams(
            dimension_semantics=("arbitrary", "arbitrary"), vmem_limit_bytes=VMEM_LIMIT),
        name="bias_tiles",
    )(rel_bias)


def _attn_kernel(lq1_ref, lk1_ref, lq2_ref, lk2_ref, qT_ref, k_ref, vT_ref, bias_ref, subln_ref, o_ref):
    i = pl.program_id(1)
    t = T_ATT
    qT = qT_ref[...]
    row = lax.broadcasted_iota(jnp.int32, qT.shape, 0)
    zero = jnp.zeros_like(qT)
    q_sub = (jnp.where(row < DA_HEAD_DIM, qT, zero), jnp.where(row >= DA_HEAD_DIM, qT, zero))

    def body(j, carry):
        off = pl.multiple_of(j * t, t)
        kt = k_ref[pl.ds(off, t), :]
        vt = vT_ref[:, pl.ds(off, t)]
        b = bias_ref[jnp.minimum(i - j, ND_ATT)]
        new = []
        for sub in range(2):
            m, l, acc = carry[sub]
            s = jnp.dot(kt, q_sub[sub], preferred_element_type=F32) + b
            m_new = jnp.maximum(m, jnp.max(s, axis=0, keepdims=True))
            alpha = jnp.exp2(m - m_new)
            p = jnp.exp2(s - m_new)
            l_new = alpha * l + jnp.sum(p, axis=0, keepdims=True)
            acc_new = alpha * acc + jnp.dot(vt, p.astype(BF16), preferred_element_type=F32)
            new.append((m_new, l_new, acc_new))
        return tuple(new)

    init1 = (jnp.full((1, t), NEG_BIG, F32), jnp.zeros((1, t), F32), jnp.zeros((DA_V_DIM, t), F32))
    (m1, l1, acc1), (m2, l2, acc2) = lax.fori_loop(0, i + 1, body, (init1, init1))

    lam = (jnp.exp(jnp.sum(lq1_ref[...] * lk1_ref[...], axis=-1, keepdims=True))
           - jnp.exp(jnp.sum(lq2_ref[...] * lk2_ref[...], axis=-1, keepdims=True)) + LAM_INIT)
    o = acc1 / l1 - lam * (acc2 / l2)
    ms = jnp.mean(o * o, axis=0, keepdims=True)
    y = o * lax.rsqrt(ms + EPS) * (subln_ref[...] * (1.0 - LAM_INIT))
    o_ref[...] = y.T.astype(BF16)


def _diff_attn(lq1, lk1, lq2, lk2, outT, outN, bias, subln):
    s = outN.shape[1]
    t = T_ATT
    vec = pl.BlockSpec((1, DA_HEAD_DIM), lambda h, i: (0, 0))
    return pl.pallas_call(
        _attn_kernel,
        grid=(DA_HEADS, s // t),
        in_specs=[
            vec, vec, vec, vec,
            pl.BlockSpec((None, DA_V_DIM, t), lambda h, i: (0, h, i)),
            pl.BlockSpec((None, s, DA_V_DIM), lambda h, i: (0, 0, h)),
            pl.BlockSpec((None, DA_V_DIM, s), lambda h, i: (1, h, 0)),
            pl.BlockSpec((ND_ATT + 1, None, t, t), lambda h, i: (0, h, 0, 0)),
            pl.BlockSpec((DA_V_DIM, 1), lambda h, i: (0, 0)),
        ],
        out_specs=pl.BlockSpec((t, DA_V_DIM), lambda h, i: (i, h)),
        out_shape=jax.ShapeDtypeStruct((s, ATTN_WIDTH), BF16),
        compiler_params=pltpu.CompilerParams(
            dimension_semantics=("arbitrary", "arbitrary"), vmem_limit_bytes=VMEM_LIMIT),
        name="diff_attn",
    )(lq1, lk1, lq2, lk2, outT, outN, outT, bias, subln)


def _rglru_kernel(xr_ref, yr_ref, cw_ref, cb_ref, wg_ref, gab_ref, gxb_ref, lam_ref, gn_ref, o_ref,
                  xpad, hstate, a_scr, b_scr, h_scr):
    step = pl.program_id(0)
    tr = TR_RNN
    c = RNN_WIDTH

    @pl.when(step == 0)
    def _():
        xpad[0:SUBLANES, :] = jnp.zeros((SUBLANES, c), F32)
        hstate[...] = jnp.zeros((1, c), F32)

    x = xr_ref[...].astype(F32)
    xpad[SUBLANES:SUBLANES + tr, :] = x
    xc = cb_ref[...] + cw_ref[CONV_WIDTH - 1:CONV_WIDTH, :] * x
    for back in range(1, CONV_WIDTH):
        w_row = cw_ref[CONV_WIDTH - 1 - back:CONV_WIDTH - back, :]
        xc = xc + w_row * xpad[SUBLANES - back:SUBLANES - back + tr, :]
    xpad[0:SUBLANES, :] = x[tr - SUBLANES:tr, :]

    xcb = xc.astype(BF16)
    ga_parts, gx_parts = [], []
    for g in range(RNN_BLOCKS):
        gg = jnp.dot(xcb[:, g * RNN_BLOCK:(g + 1) * RNN_BLOCK], wg_ref[g], preferred_element_type=F32)
        ga_parts.append(gg[:, :RNN_BLOCK])
        gx_parts.append(gg[:, RNN_BLOCK:])
    gate_a = jax.nn.sigmoid(jnp.concatenate(ga_parts, axis=-1) + gab_ref[...])
    gate_x = jax.nn.sigmoid(jnp.concatenate(gx_parts, axis=-1) + gxb_ref[...])
    z = -lam_ref[...]
    softplus = jnp.maximum(z, 0.0) + jnp.log1p(jnp.exp(-jnp.abs(z)))
    log_a = -RG_C * gate_a * softplus
    a = jnp.exp(log_a)
    u = (xc * gate_x) * jnp.sqrt(1.0 - a * a)

    groups = tr // SUBLANES
    a3 = a.reshape(groups, SUBLANES, c)
    b3 = u.reshape(groups, SUBLANES, c)
    row = lax.broadcasted_iota(jnp.int32, a3.shape, 1)
    shift = 1
    while shift < SUBLANES:
        a_sh = pltpu.roll(a3, shift, axis=1)
        b_sh = pltpu.roll(b3, shift, axis=1)
        keep = row >= shift
        b3 = jnp.where(keep, a3 * b_sh + b3, b3)
        a3 = jnp.where(keep, a3 * a_sh, a3)
        shift *= 2
    a_scr[...] = a3.reshape(tr, c)
    b_scr[...] = b3.reshape(tr, c)

    def carry_body(g, h):
        r0 = pl.multiple_of(g * SUBLANES, SUBLANES)
        out = a_scr[pl.ds(r0, SUBLANES), :] * h + b_scr[pl.ds(r0, SUBLANES), :]
        h_scr[pl.ds(r0, SUBLANES), :] = out
        return out[SUBLANES - 1:SUBLANES, :]

    hstate[...] = lax.fori_loop(0, groups, carry_body, hstate[...])

    y = yr_ref[...].astype(F32)
    gelu = 0.5 * y * (1.0 + jnp.tanh(math.sqrt(2.0 / math.pi) * (y + 0.044715 * (y * y * y))))
    o_ref[...] = (_rms(h_scr[...] * gelu) * gn_ref[...]).astype(BF16)


def _rglru(outN, conv_w, conv_b, w_gates, ga_b, gx_b, rg_lambda, gn):
    s = outN.shape[1]
    c = RNN_WIDTH
    row = pl.BlockSpec((1, c), lambda t: (0, 0))
    return pl.pallas_call(
        _rglru_kernel,
        grid=(s // TR_RNN,),
        in_specs=[
            pl.BlockSpec((None, TR_RNN, c), lambda t: (1, t, 0)),
            pl.BlockSpec((None, TR_RNN, c), lambda t: (2, t, 0)),
            pl.BlockSpec((CONV_WIDTH, c), lambda t: (0, 0)),
            row,
            pl.BlockSpec((RNN_BLOCKS, RNN_BLOCK, 2 * RNN_BLOCK), lambda t: (0, 0, 0)),
            row, row, row, row,
        ],
        out_specs=pl.BlockSpec((TR_RNN, c), lambda t: (t, 0)),
        out_shape=jax.ShapeDtypeStruct((s, c), BF16),
        scratch_shapes=[
            pltpu.VMEM((TR_RNN + SUBLANES, c), F32),
            pltpu.VMEM((1, c), F32),
            pltpu.VMEM((TR_RNN, c), F32),
            pltpu.VMEM((TR_RNN, c), F32),
            pltpu.VMEM((TR_RNN, c), F32),
        ],
        compiler_params=pltpu.CompilerParams(
            dimension_semantics=("arbitrary",), vmem_limit_bytes=VMEM_LIMIT),
        name="rglru",
    )(outN, outN, conv_w, conv_b, w_gates, ga_b, gx_b, rg_lambda, gn)


def _outproj_kernel(x_ref, oa_ref, or_ref, w_ref, o_ref):
    y = jnp.dot(oa_ref[...], w_ref[0:ATTN_WIDTH, :], preferred_element_type=F32)
    y = y + jnp.dot(or_ref[...], w_ref[ATTN_WIDTH:, :], preferred_element_type=F32)
    o_ref[...] = x_ref[...] + y


def _out_proj(x, oa, orn, w):
    s = x.shape[0]
    tm = TM_PROJ
    return pl.pallas_call(
        _outproj_kernel,
        grid=(s // tm,),
        in_specs=[
            pl.BlockSpec((tm, D_MODEL), lambda i: (i, 0)),
            pl.BlockSpec((tm, ATTN_WIDTH), lambda i: (i, 0)),
            pl.BlockSpec((tm, RNN_WIDTH), lambda i: (i, 0)),
            pl.BlockSpec((ATTN_WIDTH + RNN_WIDTH, D_MODEL), lambda i: (0, 0)),
        ],
        out_specs=pl.BlockSpec((tm, D_MODEL), lambda i: (i, 0)),
        out_shape=jax.ShapeDtypeStruct((s, D_MODEL), F32),
        compiler_params=pltpu.CompilerParams(
            dimension_semantics=("arbitrary",), vmem_limit_bytes=VMEM_LIMIT),
        name="out_proj",
    )(x, oa, orn, w)


def _memkv_kernel(mem_ref, g_ref, wk_ref, wv_ref, kg_ref, kT_ref, v_ref):
    m = (_rms(mem_ref[...]) * g_ref[...]).astype(BF16)
    k = jnp.dot(m, wk_ref[...], preferred_element_type=F32)
    parts = [_rms(k[:, h * X_HEAD_DIM:(h + 1) * X_HEAD_DIM]) for h in range(X_HEADS)]
    kn = jnp.concatenate(parts, axis=-1) * kg_ref[...]
    kT_ref[...] = kn.T.astype(BF16)
    v_ref[...] = jnp.dot(m, wv_ref[...], preferred_element_type=F32).astype(BF16)


def _mem_kv(mem, g, wk, wv, kg):
    return pl.pallas_call(
        _memkv_kernel,
        out_shape=[
            jax.ShapeDtypeStruct((X_WIDTH, MEM_LEN), BF16),
            jax.ShapeDtypeStruct((MEM_LEN, X_WIDTH), BF16),
        ],
        compiler_params=pltpu.CompilerParams(vmem_limit_bytes=VMEM_LIMIT),
        name="mem_kv",
    )(mem, g, wk, wv, kg)


def _xattn_kernel(x_ref, g_ref, wq_ref, qg_ref, kT_ref, v_ref, wo_ref, o_ref):
    x = x_ref[...]
    hx = (_rms(x) * g_ref[...]).astype(BF16)
    q = jnp.dot(hx, wq_ref[...], preferred_element_type=F32)
    outs = []
    for h in range(X_HEADS):
        sl = slice(h * X_HEAD_DIM, (h + 1) * X_HEAD_DIM)
        qn = (_rms(q[:, sl]) * qg_ref[:, sl]).astype(BF16)
        s = jnp.dot(qn, kT_ref[sl, :], preferred_element_type=F32)
        p = jnp.exp2(s - jnp.max(s, axis=-1, keepdims=True))
        l = jnp.sum(p, axis=-1, keepdims=True)
        o = jnp.dot(p.astype(BF16), v_ref[:, sl], preferred_element_type=F32) / l
        outs.append(o.astype(BF16))
    o_all = jnp.concatenate(outs, axis=-1)
    o_ref[...] = x + jnp.dot(o_all, wo_ref[...], preferred_element_type=F32)


def _xattn(x, g, wq, qg, kT, v, wo):
    s = x.shape[0]
    tm = TM_PROJ
    full = lambda shape: pl.BlockSpec(shape, lambda i: tuple(0 for _ in shape))
    return pl.pallas_call(
        _xattn_kernel,
        grid=(s // tm,),
        in_specs=[
            pl.BlockSpec((tm, D_MODEL), lambda i: (i, 0)),
            full((1, D_MODEL)),
            full((D_MODEL, X_WIDTH)),
            full((1, X_WIDTH)),
            full((X_WIDTH, MEM_LEN)),
            full((MEM_LEN, X_WIDTH)),
            full((X_WIDTH, D_MODEL)),
        ],
        out_specs=pl.BlockSpec((tm, D_MODEL), lambda i: (i, 0)),
        out_shape=jax.ShapeDtypeStruct((s, D_MODEL), F32),
        compiler_params=pltpu.CompilerParams(
            dimension_semantics=("arbitrary",), vmem_limit_bytes=VMEM_LIMIT),
        name="xattn",
    )(x, g, wq, qg, kT, v, wo)


def _ffn_kernel(x_ref, g_ref, wg_ref, wu_ref, wd_ref, o_ref, h_scr, acc_scr):
    f = pl.program_id(1)

    @pl.when(f == 0)
    def _():
        x = x_ref[...]
        h_scr[...] = (_rms(x) * g_ref[...]).astype(BF16)
        acc_scr[...] = x

    h = h_scr[...]
    gate = jnp.dot(h, wg_ref[...], preferred_element_type=F32)
    up = jnp.dot(h, wu_ref[...], preferred_element_type=F32)
    act = (gate * jax.nn.sigmoid(gate) * up).astype(BF16)
    acc_scr[...] += jnp.dot(act, wd_ref[...], preferred_element_type=F32)

    @pl.when(f == pl.num_programs(1) - 1)
    def _():
        o_ref[...] = acc_scr[...]


def _ffn(x, g, w_gate_up, w_down):
    s = x.shape[0]
    tm = TM_PROJ
    nf = D_FF // TF_FFN
    return pl.pallas_call(
        _ffn_kernel,
        grid=(s // tm, nf),
        in_specs=[
            pl.BlockSpec((tm, D_MODEL), lambda i, f: (i, 0)),
            pl.BlockSpec((1, D_MODEL), lambda i, f: (0, 0)),
            pl.BlockSpec((D_MODEL, TF_FFN), lambda i, f: (0, f)),
            pl.BlockSpec((D_MODEL, TF_FFN), lambda i, f: (0, nf + f)),
            pl.BlockSpec((TF_FFN, D_MODEL), lambda i, f: (f, 0)),
        ],
        out_specs=pl.BlockSpec((tm, D_MODEL), lambda i, f: (i, 0)),
        out_shape=jax.ShapeDtypeStruct((s, D_MODEL), F32),
        scratch_shapes=[pltpu.VMEM((tm, D_MODEL), BF16), pltpu.VMEM((tm, D_MODEL), F32)],
        compiler_params=pltpu.CompilerParams(
            dimension_semantics=("arbitrary", "arbitrary"), vmem_limit_bytes=VMEM_LIMIT),
        name="ffn",
    )(x, g, w_gate_up, w_gate_up, w_down)


def kernel(x, mem, rel_bias, attn_norm, w_in, da_q_norm, da_k_norm, da_lambda_q1, da_lambda_k1, da_lambda_q2, da_lambda_k2, da_subln, conv_w, conv_b, gate_a_w, gate_a_b, gate_x_w, gate_x_b, rg_lambda, rnn_out_norm, w_out, xattn_norm, mem_norm, xq_w, xk_w, xv_w, xq_norm, xk_norm, xo_w, ffn_norm, w_gate_up, w_down):
    assert x.shape == (1, SEQ, D_MODEL) and mem.shape == (1, MEM_LEN, D_MODEL)
    assert w_in.shape == (1, D_MODEL, 3 * ATTN_WIDTH + 2 * RNN_WIDTH)
    assert w_gate_up.shape == (1, D_MODEL, 2 * D_FF) and w_down.shape == (1, D_FF, D_MODEL)
    assert SEQ % T_ATT == 0 and SEQ % TM_PROJ == 0 and SEQ % TR_RNN == 0 and D_FF % TF_FFN == 0
    x2d = x[0]
    row = lambda v: v.reshape(1, -1).astype(F32)

    a = ATTN_WIDTH
    wi = w_in[0]
    w_in_b = jnp.concatenate([wi[:, 0:a], wi[:, 2 * a:3 * a], wi[:, a:2 * a], wi[:, 3 * a:]], axis=1).astype(BF16)
    grp = jnp.arange(MXU_DIM) // DA_HEAD_DIM
    gsum = jnp.where(grp[:, None] == grp[None, :], 1.0 / DA_HEAD_DIM, 0.0).astype(BF16)
    reps = ATTN_WIDTH // DA_HEAD_DIM
    qg = row(jnp.tile(da_q_norm[0], reps)) * (DA_HEAD_DIM ** -0.5 * LOG2E)
    kg = row(jnp.tile(da_k_norm[0], reps))

    outT, outN = _in_proj(x2d, row(attn_norm[0]), w_in_b, gsum, qg, kg)

    bias = _bias_tiles(rel_bias.astype(F32))
    o_attn = _diff_attn(row(da_lambda_q1[0]), row(da_lambda_k1[0]), row(da_lambda_q2[0]), row(da_lambda_k2[0]),
                        outT, outN, bias, da_subln[0].reshape(DA_V_DIM, 1).astype(F32))

    w_gates = jnp.concatenate([gate_a_w[0], gate_x_w[0]], axis=-1).astype(BF16)
    o_rnn = _rglru(outN, conv_w[0].astype(F32), row(conv_b[0]), w_gates, row(gate_a_b[0]), row(gate_x_b[0]),
                   row(rg_lambda[0]), row(rnn_out_norm[0]))

    x1 = _out_proj(x2d, o_attn, o_rnn, w_out[0].astype(BF16))

    kT, vm = _mem_kv(mem[0], row(mem_norm[0]), xk_w[0].astype(BF16), xv_w[0].astype(BF16),
                     row(jnp.tile(xk_norm[0], X_HEADS)))
    xqg = row(jnp.tile(xq_norm[0], X_HEADS)) * (X_HEAD_DIM ** -0.5 * LOG2E)
    x2 = _xattn(x1, row(xattn_norm[0]), xq_w[0].astype(BF16), xqg, kT, vm, xo_w[0].astype(BF16))

    x3 = _ffn(x2, row(ffn_norm[0]), w_gate_up[0].astype(BF16), w_down[0].astype(BF16))
    return x3[None]
```

```python
import functools
import math

import jax
import jax.numpy as jnp
from jax import lax
from jax.experimental import pallas as pl
from jax.experimental.pallas import tpu as pltpu

F32 = jnp.float32
BF16 = jnp.bfloat16

D_MODEL = 2048
SEQ = 16384
CHUNK = 64
CHUNK_SHIFT = CHUNK.bit_length() - 1
assert 1 << CHUNK_SHIFT == CHUNK
MEM_LEN = 256
EPS = 1e-6
ATTN_WIDTH = 1024
DA_HEAD_DIM = 64
DA_HEADS = 8
DA_V_DIM = 2 * DA_HEAD_DIM
RNN_WIDTH = 1024
RNN_BLOCKS = 8
RNN_BLOCK = RNN_WIDTH // RNN_BLOCKS
CONV_WIDTH = 4
RG_C = 8.0
X_HEADS = 4
X_HEAD_DIM = 128
X_WIDTH = X_HEADS * X_HEAD_DIM
D_FF = 5632
NUM_BUCKETS = 32
MAX_DISTANCE = 1024
LAM_INIT = 0.8 - 0.6 * math.exp(-0.3 * 0)
LOG2E = 1.4426950408889634
NEG_BIG = -1e30
MAX_EXP2_SPREAD = 120.0

LANES = 128
SUBLANES = 8
MXU_DIM = 256
VMEM_LIMIT = 56 * 1024 * 1024

TM_PROJ = 512
SEG = 1024
T_ATT = 512
TR_RNN = 256
TF_FFN = 512


def _far_bucket_distance():
    half = NUM_BUCKETS // 2
    max_exact = half // 2
    n = max_exact
    while max_exact + int(math.log(n / max_exact) / math.log(MAX_DISTANCE / max_exact) * (half - max_exact)) < half - 1:
        n += 1
    return n


FAR_DIST = _far_bucket_distance()
ND_ATT = next(d for d in range(1, 64) if (d - 1) * T_ATT + 1 >= FAR_DIST + 64)
FAR_BUCKET = NUM_BUCKETS // 2 - 1
N_BIAS_TILES = ND_ATT + 2
ATT_UNROLL = 2


def _rms(x, eps=EPS):
    return x * lax.rsqrt(jnp.mean(x * x, axis=-1, keepdims=True) + eps)


def _inproj_kernel(x_ref, g_ref, w_ref, gsum_ref, qg_ref, kg_ref, outT_ref, outN_ref, h_scr):
    j = pl.program_id(1)

    @pl.when(j == 0)
    def _():
        h_scr[...] = (_rms(x_ref[...]) * g_ref[...]).astype(BF16)

    acc = jnp.dot(h_scr[...], w_ref[...], preferred_element_type=F32)

    def group_norm(a, gain_row):
        sq = (a * a).astype(BF16)
        parts = [jnp.dot(sq[:, c * MXU_DIM:(c + 1) * MXU_DIM], gsum_ref[...], preferred_element_type=F32)
                 for c in range(SEG // MXU_DIM)]
        ms = jnp.concatenate(parts, axis=-1)
        return a * lax.rsqrt(ms + EPS) * gain_row

    @pl.when(j == 0)
    def _():
        outT_ref[...] = group_norm(acc, qg_ref[...]).T.astype(BF16)

    @pl.when(j == 1)
    def _():
        outT_ref[...] = acc.T.astype(BF16)

    @pl.when(j == 2)
    def _():
        outN_ref[...] = group_norm(acc, kg_ref[...]).astype(BF16)

    @pl.when(j >= 3)
    def _():
        outN_ref[...] = acc.astype(BF16)


def _in_proj(x, g, w, gsum, qg, kg):
    s = x.shape[0]
    nseg = w.shape[1] // SEG
    return pl.pallas_call(
        _inproj_kernel,
        grid=(s // TM_PROJ, nseg),
        in_specs=[
            pl.BlockSpec((TM_PROJ, D_MODEL), lambda i, j: (i, 0)),
            pl.BlockSpec((1, D_MODEL), lambda i, j: (0, 0)),
            pl.BlockSpec((D_MODEL, SEG), lambda i, j: (0, j)),
            pl.BlockSpec((MXU_DIM, MXU_DIM), lambda i, j: (0, 0)),
            pl.BlockSpec((1, SEG), lambda i, j: (0, 0)),
            pl.BlockSpec((1, SEG), lambda i, j: (0, 0)),
        ],
        out_specs=[
            pl.BlockSpec((None, SEG, TM_PROJ), lambda i, j: (jnp.minimum(j, 1), 0, i)),
            pl.BlockSpec((None, TM_PROJ, SEG), lambda i, j: (jnp.maximum(j - 2, 0), i, 0)),
        ],
        out_shape=[
            jax.ShapeDtypeStruct((2, SEG, s), BF16),
            jax.ShapeDtypeStruct((3, s, SEG), BF16),
        ],
        scratch_shapes=[pltpu.VMEM((TM_PROJ, D_MODEL), BF16)],
        compiler_params=pltpu.CompilerParams(
            dimension_semantics=("arbitrary", "arbitrary"), vmem_limit_bytes=VMEM_LIMIT),
        name="in_proj",
    )(x, g, w, gsum, qg, kg)


def _bias_kernel(rb_ref, shift_ref, out_ref):
    d = pl.program_id(0) - 1
    h = pl.program_id(1)
    t = T_ATT
    kk = lax.broadcasted_iota(jnp.int32, (t, t), 0)
    qq = lax.broadcasted_iota(jnp.int32, (t, t), 1)
    rel = kk - qq - d * t
    half = NUM_BUCKETS // 2
    max_exact = half // 2
    n = jnp.abs(rel)
    large = max_exact + (jnp.log(jnp.maximum(n, 1).astype(F32) / max_exact)
                         / math.log(MAX_DISTANCE / max_exact) * (half - max_exact)).astype(jnp.int32)
    large = jnp.minimum(large, half - 1)
    bucket = jnp.where(rel > 0, half, 0) + jnp.where(n < max_exact, n, large)
    val = jnp.zeros((t, t), F32)
    for b in range(NUM_BUCKETS):
        val = jnp.where(bucket == b, rb_ref[b, h], val)
    val = (val - rb_ref[FAR_BUCKET, h]) * LOG2E - shift_ref[h]
    allowed = (kk >> CHUNK_SHIFT) <= ((qq + d * t) >> CHUNK_SHIFT)
    out_ref[...] = jnp.where(allowed, val, NEG_BIG)


def _bias_tiles(rel_bias, shift):
    return pl.pallas_call(
        _bias_kernel,
        grid=(N_BIAS_TILES, DA_HEADS),
        in_specs=[pl.BlockSpec(memory_space=pltpu.SMEM), pl.BlockSpec(memory_space=pltpu.SMEM)],
        out_specs=pl.BlockSpec((None, None, T_ATT, T_ATT), lambda d, h: (d, h, 0, 0)),
        out_shape=jax.ShapeDtypeStruct((N_BIAS_TILES, DA_HEADS, T_ATT, T_ATT), F32),
        compiler_params=pltpu.CompilerParams(
            dimension_semantics=("arbitrary", "arbitrary"), vmem_limit_bytes=VMEM_LIMIT),
        name="bias_tiles",
    )(rel_bias, shift)


def _logit_bound(rel_bias, q_gain, k_gain):
    rb = rel_bias.astype(F32)
    b2 = (rb - rb[FAR_BUCKET]) * LOG2E
    qk = (LOG2E * DA_HEAD_DIM ** 0.5 * (1.0 + 2.0 ** -6)) * jnp.max(jnp.abs(q_gain)) * jnp.max(jnp.abs(k_gain))
    bound = qk + jnp.max(b2, axis=0)
    spread = 2.0 * qk + jnp.max(jnp.max(b2, axis=0) - jnp.min(b2, axis=0))
    bounded = spread <= MAX_EXP2_SPREAD
    return jnp.where(bounded, bound, 0.0).astype(F32), bounded.astype(jnp.int32).reshape(1)


def _attn_kernel(bounded_ref, lq1_ref, lk1_ref, lq2_ref, lk2_ref, qT_ref, k_ref, vT_ref, bias_ref, subln_ref,
                 o_ref):
    i = pl.program_id(1)
    t = T_ATT
    qT = qT_ref[...]
    row = lax.broadcasted_iota(jnp.int32, qT.shape, 0)
    zero = jnp.zeros_like(qT)
    q_sub = (jnp.where(row < DA_HEAD_DIM, qT, zero), jnp.where(row >= DA_HEAD_DIM, qT, zero))

    def tiles(j):
        off = pl.multiple_of(jnp.minimum(j, pl.num_programs(1) - 1) * t, t)
        kt = k_ref[pl.ds(off, t), :]
        vt = vT_ref[:, pl.ds(off, t)]
        b = bias_ref[jnp.clip(i - j, -1, ND_ATT) + 1]
        return kt, vt, b

    def finish(acc1, l1, acc2, l2):
        lam = (jnp.exp(jnp.sum(lq1_ref[...] * lk1_ref[...], axis=-1, keepdims=True))
               - jnp.exp(jnp.sum(lq2_ref[...] * lk2_ref[...], axis=-1, keepdims=True)) + LAM_INIT)
        o = acc1 / l1 - lam * (acc2 / l2)
        ms = jnp.mean(o * o, axis=0, keepdims=True)
        y = o * lax.rsqrt(ms + EPS) * (subln_ref[...] * (1.0 - LAM_INIT))
        o_ref[...] = y.T.astype(BF16)

    @pl.when(bounded_ref[0] == 1)
    def _():
        def probs(j, sub, l8):
            kt, _, b = tiles(j)
            p = jnp.exp2(jnp.dot(kt, q_sub[sub], preferred_element_type=F32) + b)
            l8 = l8 + jnp.sum(p.reshape(t // SUBLANES, SUBLANES, t), axis=0)
            return p.astype(BF16), l8

        def weighted(j, p, acc):
            _, vt, _ = tiles(j)
            return acc + jnp.dot(vt, p, preferred_element_type=F32)

        def body(r, carry):
            carry = list(carry)
            for u in range(ATT_UNROLL):
                j = 1 + ATT_UNROLL * r + u
                for sub in range(2):
                    p_prev, l8, acc = carry[sub]
                    acc = weighted(j - 1, p_prev, acc)
                    p, l8 = probs(j, sub, l8)
                    carry[sub] = (p, l8, acc)
            return tuple(carry)

        init = []
        for sub in range(2):
            p, l8 = probs(0, sub, jnp.zeros((SUBLANES, t), F32))
            init.append((p, l8, jnp.zeros((DA_V_DIM, t), F32)))
        rounds = (i + ATT_UNROLL - 1) // ATT_UNROLL
        last = ATT_UNROLL * rounds
        (p1, l1, acc1), (p2, l2, acc2) = lax.fori_loop(0, rounds, body, tuple(init))
        acc1 = weighted(last, p1, acc1)
        acc2 = weighted(last, p2, acc2)
        finish(acc1, jnp.sum(l1, axis=0, keepdims=True), acc2, jnp.sum(l2, axis=0, keepdims=True))

    @pl.when(bounded_ref[0] != 1)
    def _():
        def body(j, carry):
            kt, vt, b = tiles(j)
            new = []
            for sub in range(2):
                m, l, acc = carry[sub]
                s = jnp.dot(kt, q_sub[sub], preferred_element_type=F32) + b
                m_new = jnp.maximum(m, jnp.max(s, axis=0, keepdims=True))
                alpha = jnp.exp2(m - m_new)
                p = jnp.exp2(s - m_new)
                l_new = alpha * l + jnp.sum(p, axis=0, keepdims=True)
                acc_new = alpha * acc + jnp.dot(vt, p.astype(BF16), preferred_element_type=F32)
                new.append((m_new, l_new, acc_new))
            return tuple(new)

        init1 = (jnp.full((1, t), NEG_BIG, F32), jnp.zeros((1, t), F32), jnp.zeros((DA_V_DIM, t), F32))
        (_, l1, acc1), (_, l2, acc2) = lax.fori_loop(0, i + 1, body, (init1, init1))
        finish(acc1, l1, acc2, l2)


def _diff_attn(bounded, lq1, lk1, lq2, lk2, outT, outN, bias, subln):
    s = outN.shape[1]
    t = T_ATT
    vec = pl.BlockSpec((1, DA_HEAD_DIM), lambda h, i: (0, 0))
    return pl.pallas_call(
        _attn_kernel,
        grid=(DA_HEADS, s // t),
        in_specs=[
            pl.BlockSpec(memory_space=pltpu.SMEM),
            vec, vec, vec, vec,
            pl.BlockSpec((None, DA_V_DIM, t), lambda h, i: (0, h, i)),
            pl.BlockSpec((None, s, DA_V_DIM), lambda h, i: (0, 0, h)),
            pl.BlockSpec((None, DA_V_DIM, s), lambda h, i: (1, h, 0)),
            pl.BlockSpec((N_BIAS_TILES, None, t, t), lambda h, i: (0, h, 0, 0)),
            pl.BlockSpec((DA_V_DIM, 1), lambda h, i: (0, 0)),
        ],
        out_specs=pl.BlockSpec((t, DA_V_DIM), lambda h, i: (i, h)),
        out_shape=jax.ShapeDtypeStruct((s, ATTN_WIDTH), BF16),
        compiler_params=pltpu.CompilerParams(
            dimension_semantics=("arbitrary", "arbitrary"), vmem_limit_bytes=VMEM_LIMIT),
        name="diff_attn",
    )(bounded, lq1, lk1, lq2, lk2, outT, outN, outT, bias, subln)


def _rglru_kernel(xr_ref, yr_ref, cw_ref, cb_ref, wg_ref, gab_ref, gxb_ref, lam_ref, gn_ref, o_ref,
                  xpad, hstate, a_scr, b_scr, h_scr):
    step = pl.program_id(0)
    tr = TR_RNN
    c = RNN_WIDTH

    @pl.when(step == 0)
    def _():
        xpad[0:SUBLANES, :] = jnp.zeros((SUBLANES, c), F32)
        hstate[...] = jnp.zeros((1, c), F32)

    x = xr_ref[...].astype(F32)
    xpad[SUBLANES:SUBLANES + tr, :] = x
    xc = cb_ref[...] + cw_ref[CONV_WIDTH - 1:CONV_WIDTH, :] * x
    for back in range(1, CONV_WIDTH):
        w_row = cw_ref[CONV_WIDTH - 1 - back:CONV_WIDTH - back, :]
        xc = xc + w_row * xpad[SUBLANES - back:SUBLANES - back + tr, :]
    xpad[0:SUBLANES, :] = x[tr - SUBLANES:tr, :]

    xcb = xc.astype(BF16)
    ga_parts, gx_parts = [], []
    for g in range(RNN_BLOCKS):
        gg = jnp.dot(xcb[:, g * RNN_BLOCK:(g + 1) * RNN_BLOCK], wg_ref[g], preferred_element_type=F32)
        ga_parts.append(gg[:, :RNN_BLOCK])
        gx_parts.append(gg[:, RNN_BLOCK:])
    gate_a = jax.nn.sigmoid(jnp.concatenate(ga_parts, axis=-1) + gab_ref[...])
    gate_x = jax.nn.sigmoid(jnp.concatenate(gx_parts, axis=-1) + gxb_ref[...])
    z = -lam_ref[...]
    softplus = jnp.maximum(z, 0.0) + jnp.log1p(jnp.exp(-jnp.abs(z)))
    log_a = -RG_C * gate_a * softplus
    a = jnp.exp(log_a)
    u = (xc * gate_x) * jnp.sqrt(1.0 - a * a)

    groups = tr // SUBLANES
    a3 = a.reshape(groups, SUBLANES, c)
    b3 = u.reshape(groups, SUBLANES, c)
    row = lax.broadcasted_iota(jnp.int32, a3.shape, 1)
    shift = 1
    while shift < SUBLANES:
        a_sh = pltpu.roll(a3, shift, axis=1)
        b_sh = pltpu.roll(b3, shift, axis=1)
        keep = row >= shift
        b3 = jnp.where(keep, a3 * b_sh + b3, b3)
        a3 = jnp.where(keep, a3 * a_sh, a3)
        shift *= 2
    a_scr[...] = a3.reshape(tr, c)
    b_scr[...] = b3.reshape(tr, c)

    def carry_body(g, h):
        r0 = pl.multiple_of(g * SUBLANES, SUBLANES)
        out = a_scr[pl.ds(r0, SUBLANES), :] * h + b_scr[pl.ds(r0, SUBLANES), :]
        h_scr[pl.ds(r0, SUBLANES), :] = out
        return out[SUBLANES - 1:SUBLANES, :]

    hstate[...] = lax.fori_loop(0, groups, carry_body, hstate[...])

    y = yr_ref[...].astype(F32)
    gelu = 0.5 * y * (1.0 + jnp.tanh(math.sqrt(2.0 / math.pi) * (y + 0.044715 * (y * y * y))))
    o_ref[...] = (_rms(h_scr[...] * gelu) * gn_ref[...]).astype(BF16)


def _rglru(outN, conv_w, conv_b, w_gates, ga_b, gx_b, rg_lambda, gn):
    s = outN.shape[1]
    c = RNN_WIDTH
    row = pl.BlockSpec((1, c), lambda t: (0, 0))
    return pl.pallas_call(
        _rglru_kernel,
        grid=(s // TR_RNN,),
        in_specs=[
            pl.BlockSpec((None, TR_RNN, c), lambda t: (1, t, 0)),
            pl.BlockSpec((None, TR_RNN, c), lambda t: (2, t, 0)),
            pl.BlockSpec((CONV_WIDTH, c), lambda t: (0, 0)),
            row,
            pl.BlockSpec((RNN_BLOCKS, RNN_BLOCK, 2 * RNN_BLOCK), lambda t: (0, 0, 0)),
            row, row, row, row,
        ],
        out_specs=pl.BlockSpec((TR_RNN, c), lambda t: (t, 0)),
        out_shape=jax.ShapeDtypeStruct((s, c), BF16),
        scratch_shapes=[
            pltpu.VMEM((TR_RNN + SUBLANES, c), F32),
            pltpu.VMEM((1, c), F32),
            pltpu.VMEM((TR_RNN, c), F32),
            pltpu.VMEM((TR_RNN, c), F32),
            pltpu.VMEM((TR_RNN, c), F32),
        ],
        compiler_params=pltpu.CompilerParams(
            dimension_semantics=("arbitrary",), vmem_limit_bytes=VMEM_LIMIT),
        name="rglru",
    )(outN, outN, conv_w, conv_b, w_gates, ga_b, gx_b, rg_lambda, gn)


def _outproj_kernel(x_ref, oa_ref, or_ref, w_ref, o_ref):
    y = jnp.dot(oa_ref[...], w_ref[0:ATTN_WIDTH, :], preferred_element_type=F32)
    y = y + jnp.dot(or_ref[...], w_ref[ATTN_WIDTH:, :], preferred_element_type=F32)
    o_ref[...] = x_ref[...] + y


def _out_proj(x, oa, orn, w):
    s = x.shape[0]
    tm = TM_PROJ
    return pl.pallas_call(
        _outproj_kernel,
        grid=(s // tm,),
        in_specs=[
            pl.BlockSpec((tm, D_MODEL), lambda i: (i, 0)),
            pl.BlockSpec((tm, ATTN_WIDTH), lambda i: (i, 0)),
            pl.BlockSpec((tm, RNN_WIDTH), lambda i: (i, 0)),
            pl.BlockSpec((ATTN_WIDTH + RNN_WIDTH, D_MODEL), lambda i: (0, 0)),
        ],
        out_specs=pl.BlockSpec((tm, D_MODEL), lambda i: (i, 0)),
        out_shape=jax.ShapeDtypeStruct((s, D_MODEL), F32),
        compiler_params=pltpu.CompilerParams(
            dimension_semantics=("arbitrary",), vmem_limit_bytes=VMEM_LIMIT),
        name="out_proj",
    )(x, oa, orn, w)


def _memkv_kernel(mem_ref, g_ref, wk_ref, wv_ref, kg_ref, kT_ref, v_ref):
    m = (_rms(mem_ref[...]) * g_ref[...]).astype(BF16)
    k = jnp.dot(m, wk_ref[...], preferred_element_type=F32)
    parts = [_rms(k[:, h * X_HEAD_DIM:(h + 1) * X_HEAD_DIM]) for h in range(X_HEADS)]
    kn = jnp.concatenate(parts, axis=-1) * kg_ref[...]
    kT_ref[...] = kn.T.astype(BF16)
    v_ref[...] = jnp.dot(m, wv_ref[...], preferred_element_type=F32).astype(BF16)


def _mem_kv(mem, g, wk, wv, kg):
    return pl.pallas_call(
        _memkv_kernel,
        out_shape=[
            jax.ShapeDtypeStruct((X_WIDTH, MEM_LEN), BF16),
            jax.ShapeDtypeStruct((MEM_LEN, X_WIDTH), BF16),
        ],
        compiler_params=pltpu.CompilerParams(vmem_limit_bytes=VMEM_LIMIT),
        name="mem_kv",
    )(mem, g, wk, wv, kg)


def _xattn_kernel(x_ref, g_ref, wq_ref, qg_ref, kT_ref, v_ref, wo_ref, o_ref):
    x = x_ref[...]
    hx = (_rms(x) * g_ref[...]).astype(BF16)
    q = jnp.dot(hx, wq_ref[...], preferred_element_type=F32)
    outs = []
    for h in range(X_HEADS):
        sl = slice(h * X_HEAD_DIM, (h + 1) * X_HEAD_DIM)
        qn = (_rms(q[:, sl]) * qg_ref[:, sl]).astype(BF16)
        s = jnp.dot(qn, kT_ref[sl, :], preferred_element_type=F32)
        p = jnp.exp2(s - jnp.max(s, axis=-1, keepdims=True))
        l = jnp.sum(p, axis=-1, keepdims=True)
        o = jnp.dot(p.astype(BF16), v_ref[:, sl], preferred_element_type=F32) / l
        outs.append(o.astype(BF16))
    o_all = jnp.concatenate(outs, axis=-1)
    o_ref[...] = x + jnp.dot(o_all, wo_ref[...], preferred_element_type=F32)


def _xattn(x, g, wq, qg, kT, v, wo):
    s = x.shape[0]
    tm = TM_PROJ
    full = lambda shape: pl.BlockSpec(shape, lambda i: tuple(0 for _ in shape))
    return pl.pallas_call(
        _xattn_kernel,
        grid=(s // tm,),
        in_specs=[
            pl.BlockSpec((tm, D_MODEL), lambda i: (i, 0)),
            full((1, D_MODEL)),
            full((D_MODEL, X_WIDTH)),
            full((1, X_WIDTH)),
            full((X_WIDTH, MEM_LEN)),
            full((MEM_LEN, X_WIDTH)),
            full((X_WIDTH, D_MODEL)),
        ],
        out_specs=pl.BlockSpec((tm, D_MODEL), lambda i: (i, 0)),
        out_shape=jax.ShapeDtypeStruct((s, D_MODEL), F32),
        compiler_params=pltpu.CompilerParams(
            dimension_semantics=("arbitrary",), vmem_limit_bytes=VMEM_LIMIT),
        name="xattn",
    )(x, g, wq, qg, kT, v, wo)


def _ffn_kernel(x_ref, g_ref, wg_ref, wu_ref, wd_ref, o_ref, h_scr, acc_scr):
    f = pl.program_id(1)

    @pl.when(f == 0)
    def _():
        x = x_ref[...]
        h_scr[...] = (_rms(x) * g_ref[...]).astype(BF16)
        acc_scr[...] = x

    h = h_scr[...]
    gate = jnp.dot(h, wg_ref[...], preferred_element_type=F32)
    up = jnp.dot(h, wu_ref[...], preferred_element_type=F32)
    act = (gate * jax.nn.sigmoid(gate) * up).astype(BF16)
    acc_scr[...] += jnp.dot(act, wd_ref[...], preferred_element_type=F32)

    @pl.when(f == pl.num_programs(1) - 1)
    def _():
        o_ref[...] = acc_scr[...]


def _ffn(x, g, w_gate_up, w_down):
    s = x.shape[0]
    tm = TM_PROJ
    nf = D_FF // TF_FFN
    return pl.pallas_call(
        _ffn_kernel,
        grid=(s // tm, nf),
        in_specs=[
            pl.BlockSpec((tm, D_MODEL), lambda i, f: (i, 0)),
            pl.BlockSpec((1, D_MODEL), lambda i, f: (0, 0)),
            pl.BlockSpec((D_MODEL, TF_FFN), lambda i, f: (0, f)),
            pl.BlockSpec((D_MODEL, TF_FFN), lambda i, f: (0, nf + f)),
            pl.BlockSpec((TF_FFN, D_MODEL), lambda i, f: (f, 0)),
        ],
        out_specs=pl.BlockSpec((tm, D_MODEL), lambda i, f: (i, 0)),
        out_shape=jax.ShapeDtypeStruct((s, D_MODEL), F32),
        scratch_shapes=[pltpu.VMEM((tm, D_MODEL), BF16), pltpu.VMEM((tm, D_MODEL), F32)],
        compiler_params=pltpu.CompilerParams(
            dimension_semantics=("arbitrary", "arbitrary"), vmem_limit_bytes=VMEM_LIMIT),
        name="ffn",
    )(x, g, w_gate_up, w_gate_up, w_down)


def kernel(x, mem, rel_bias, attn_norm, w_in, da_q_norm, da_k_norm, da_lambda_q1, da_lambda_k1, da_lambda_q2, da_lambda_k2, da_subln, conv_w, conv_b, gate_a_w, gate_a_b, gate_x_w, gate_x_b, rg_lambda, rnn_out_norm, w_out, xattn_norm, mem_norm, xq_w, xk_w, xv_w, xq_norm, xk_norm, xo_w, ffn_norm, w_gate_up, w_down):
    assert x.shape == (1, SEQ, D_MODEL) and mem.shape == (1, MEM_LEN, D_MODEL)
    assert w_in.shape == (1, D_MODEL, 3 * ATTN_WIDTH + 2 * RNN_WIDTH)
    assert w_gate_up.shape == (1, D_MODEL, 2 * D_FF) and w_down.shape == (1, D_FF, D_MODEL)
    assert SEQ % T_ATT == 0 and SEQ % TM_PROJ == 0 and SEQ % TR_RNN == 0 and D_FF % TF_FFN == 0
    x2d = x[0]
    row = lambda v: v.reshape(1, -1).astype(F32)

    a = ATTN_WIDTH
    wi = w_in[0]
    w_in_b = jnp.concatenate([wi[:, 0:a], wi[:, 2 * a:3 * a], wi[:, a:2 * a], wi[:, 3 * a:]], axis=1).astype(BF16)
    grp = jnp.arange(MXU_DIM) // DA_HEAD_DIM
    gsum = jnp.where(grp[:, None] == grp[None, :], 1.0 / DA_HEAD_DIM, 0.0).astype(BF16)
    reps = ATTN_WIDTH // DA_HEAD_DIM
    qg = row(jnp.tile(da_q_norm[0], reps)) * (DA_HEAD_DIM ** -0.5 * LOG2E)
    kg = row(jnp.tile(da_k_norm[0], reps))

    outT, outN = _in_proj(x2d, row(attn_norm[0]), w_in_b, gsum, qg, kg)

    shift, bounded = _logit_bound(rel_bias, da_q_norm[0], da_k_norm[0])
    bias = _bias_tiles(rel_bias.astype(F32), shift)
    o_attn = _diff_attn(bounded, row(da_lambda_q1[0]), row(da_lambda_k1[0]), row(da_lambda_q2[0]), row(da_lambda_k2[0]),
                        outT, outN, bias, da_subln[0].reshape(DA_V_DIM, 1).astype(F32))

    w_gates = jnp.concatenate([gate_a_w[0], gate_x_w[0]], axis=-1).astype(BF16)
    o_rnn = _rglru(outN, conv_w[0].astype(F32), row(conv_b[0]), w_gates, row(gate_a_b[0]), row(gate_x_b[0]),
                   row(rg_lambda[0]), row(rnn_out_norm[0]))

    x1 = _out_proj(x2d, o_attn, o_rnn, w_out[0].astype(BF16))

    kT, vm = _mem_kv(mem[0], row(mem_norm[0]), xk_w[0].astype(BF16), xv_w[0].astype(BF16),
                     row(jnp.tile(xk_norm[0], X_HEADS)))
    xqg = row(jnp.tile(xq_norm[0], X_HEADS)) * (X_HEAD_DIM ** -0.5 * LOG2E)
    x2 = _xattn(x1, row(xattn_norm[0]), xq_w[0].astype(BF16), xqg, kT, vm, xo_w[0].astype(BF16))

    x3 = _ffn(x2, row(ffn_norm[0]), w_gate_up[0].astype(BF16), w_down[0].astype(BF16))
    return x3[None]
```

```python
import functools
import math

import jax
import jax.numpy as jnp
from jax import lax
from jax.experimental import pallas as pl
from jax.experimental.pallas import tpu as pltpu

F32 = jnp.float32
BF16 = jnp.bfloat16

D_MODEL = 2048
SEQ = 16384
CHUNK = 64
CHUNK_SHIFT = CHUNK.bit_length() - 1
assert 1 << CHUNK_SHIFT == CHUNK
MEM_LEN = 256
EPS = 1e-6
ATTN_WIDTH = 1024
DA_HEAD_DIM = 64
DA_HEADS = 8
DA_V_DIM = 2 * DA_HEAD_DIM
RNN_WIDTH = 1024
RNN_BLOCKS = 8
RNN_BLOCK = RNN_WIDTH // RNN_BLOCKS
CONV_WIDTH = 4
RG_C = 8.0
X_HEADS = 4
X_HEAD_DIM = 128
X_WIDTH = X_HEADS * X_HEAD_DIM
D_FF = 5632
NUM_BUCKETS = 32
MAX_DISTANCE = 1024
LAM_INIT = 0.8 - 0.6 * math.exp(-0.3 * 0)
LOG2E = 1.4426950408889634
NEG_BIG = -1e30
MAX_EXP2_SPREAD = 120.0

LANES = 128
SUBLANES = 8
MXU_DIM = 256
VMEM_LIMIT = 56 * 1024 * 1024

TM_PROJ = 512
SEG = 1024
T_ATT = 512
TR_RNN = 256
TF_FFN = 512


def _far_bucket_distance():
    half = NUM_BUCKETS // 2
    max_exact = half // 2
    n = max_exact
    while max_exact + int(math.log(n / max_exact) / math.log(MAX_DISTANCE / max_exact) * (half - max_exact)) < half - 1:
        n += 1
    return n


FAR_DIST = _far_bucket_distance()
ND_ATT = next(d for d in range(1, 64) if (d - 1) * T_ATT + 1 >= FAR_DIST + 64)
FAR_BUCKET = NUM_BUCKETS // 2 - 1
N_BIAS_TILES = ND_ATT + 2
ATT_UNROLL = 4


def _rms(x, eps=EPS):
    return x * lax.rsqrt(jnp.mean(x * x, axis=-1, keepdims=True) + eps)


def _inproj_kernel(x_ref, g_ref, w_ref, gsum_ref, qg_ref, kg_ref, outT_ref, outN_ref, h_scr):
    j = pl.program_id(1)

    @pl.when(j == 0)
    def _():
        h_scr[...] = (_rms(x_ref[...]) * g_ref[...]).astype(BF16)

    acc = jnp.dot(h_scr[...], w_ref[...], preferred_element_type=F32)

    def group_norm(a, gain_row):
        sq = (a * a).astype(BF16)
        parts = [jnp.dot(sq[:, c * MXU_DIM:(c + 1) * MXU_DIM], gsum_ref[...], preferred_element_type=F32)
                 for c in range(SEG // MXU_DIM)]
        ms = jnp.concatenate(parts, axis=-1)
        return a * lax.rsqrt(ms + EPS) * gain_row

    @pl.when(j == 0)
    def _():
        outT_ref[...] = group_norm(acc, qg_ref[...]).T.astype(BF16)

    @pl.when(j == 1)
    def _():
        outT_ref[...] = acc.T.astype(BF16)

    @pl.when(j == 2)
    def _():
        outN_ref[...] = group_norm(acc, kg_ref[...]).astype(BF16)

    @pl.when(j >= 3)
    def _():
        outN_ref[...] = acc.astype(BF16)


def _in_proj(x, g, w, gsum, qg, kg):
    s = x.shape[0]
    nseg = w.shape[1] // SEG
    return pl.pallas_call(
        _inproj_kernel,
        grid=(s // TM_PROJ, nseg),
        in_specs=[
            pl.BlockSpec((TM_PROJ, D_MODEL), lambda i, j: (i, 0)),
            pl.BlockSpec((1, D_MODEL), lambda i, j: (0, 0)),
            pl.BlockSpec((D_MODEL, SEG), lambda i, j: (0, j)),
            pl.BlockSpec((MXU_DIM, MXU_DIM), lambda i, j: (0, 0)),
            pl.BlockSpec((1, SEG), lambda i, j: (0, 0)),
            pl.BlockSpec((1, SEG), lambda i, j: (0, 0)),
        ],
        out_specs=[
            pl.BlockSpec((None, SEG, TM_PROJ), lambda i, j: (jnp.minimum(j, 1), 0, i)),
            pl.BlockSpec((None, TM_PROJ, SEG), lambda i, j: (jnp.maximum(j - 2, 0), i, 0)),
        ],
        out_shape=[
            jax.ShapeDtypeStruct((2, SEG, s), BF16),
            jax.ShapeDtypeStruct((3, s, SEG), BF16),
        ],
        scratch_shapes=[pltpu.VMEM((TM_PROJ, D_MODEL), BF16)],
        compiler_params=pltpu.CompilerParams(
            dimension_semantics=("arbitrary", "arbitrary"), vmem_limit_bytes=VMEM_LIMIT),
        name="in_proj",
    )(x, g, w, gsum, qg, kg)


def _bias_kernel(rb_ref, shift_ref, out_ref, bucket_scr):
    d = pl.program_id(0) - 1
    t = T_ATT
    half = NUM_BUCKETS // 2

    def fill(n_buckets):
        def head(h, carry):
            bucket = bucket_scr[...]
            val = jnp.zeros((t, t), F32)
            for b in range(n_buckets):
                val = jnp.where(bucket == b, rb_ref[b, h], val)
            val = (val - rb_ref[FAR_BUCKET, h]) * LOG2E - shift_ref[h]
            out_ref[h] = jnp.where(bucket >= 0, val, NEG_BIG)
            return carry
        lax.fori_loop(0, DA_HEADS, head, 0)

    @pl.when(jnp.logical_and(d >= 0, d < ND_ATT))
    def _():
        kk = lax.broadcasted_iota(jnp.int32, (t, t), 0)
        qq = lax.broadcasted_iota(jnp.int32, (t, t), 1)
        rel = kk - qq - d * t
        max_exact = half // 2
        n = jnp.abs(rel)
        large = max_exact + (jnp.log(jnp.maximum(n, 1).astype(F32) / max_exact)
                             / math.log(MAX_DISTANCE / max_exact) * (half - max_exact)).astype(jnp.int32)
        large = jnp.minimum(large, half - 1)
        bucket = jnp.where(rel > 0, half, 0) + jnp.where(n < max_exact, n, large)
        allowed = (kk >> CHUNK_SHIFT) <= ((qq + d * t) >> CHUNK_SHIFT)
        bucket_scr[...] = jnp.where(allowed, bucket, -1)

    @pl.when(d == 0)
    def _():
        fill(NUM_BUCKETS)

    @pl.when(jnp.logical_and(d >= 1, d < ND_ATT))
    def _():
        fill(half)

    @pl.when(jnp.logical_or(d < 0, d >= ND_ATT))
    def _():
        def head(h, carry):
            out_ref[h] = jnp.full((t, t), jnp.where(d < 0, NEG_BIG, -shift_ref[h]), F32)
            return carry
        lax.fori_loop(0, DA_HEADS, head, 0)


def _bias_tiles(rel_bias, shift):
    return pl.pallas_call(
        _bias_kernel,
        grid=(N_BIAS_TILES,),
        in_specs=[pl.BlockSpec(memory_space=pltpu.SMEM), pl.BlockSpec(memory_space=pltpu.SMEM)],
        out_specs=pl.BlockSpec((None, DA_HEADS, T_ATT, T_ATT), lambda d: (d, 0, 0, 0)),
        out_shape=jax.ShapeDtypeStruct((N_BIAS_TILES, DA_HEADS, T_ATT, T_ATT), F32),
        scratch_shapes=[pltpu.VMEM((T_ATT, T_ATT), jnp.int32)],
        compiler_params=pltpu.CompilerParams(
            dimension_semantics=("arbitrary",), vmem_limit_bytes=VMEM_LIMIT),
        name="bias_tiles",
    )(rel_bias, shift)


def _logit_bound(rel_bias, q_gain, k_gain):
    rb = rel_bias.astype(F32)
    b2 = (rb - rb[FAR_BUCKET]) * LOG2E
    qk = (LOG2E * DA_HEAD_DIM ** 0.5 * (1.0 + 2.0 ** -6)) * jnp.max(jnp.abs(q_gain)) * jnp.max(jnp.abs(k_gain))
    bound = qk + jnp.max(b2, axis=0)
    spread = 2.0 * qk + jnp.max(jnp.max(b2, axis=0) - jnp.min(b2, axis=0))
    bounded = spread <= MAX_EXP2_SPREAD
    return jnp.where(bounded, bound, 0.0).astype(F32), bounded.astype(jnp.int32).reshape(1)


def _attn_kernel(bounded_ref, lq1_ref, lk1_ref, lq2_ref, lk2_ref, qT_ref, k_ref, vT_ref, bias_ref, subln_ref,
                 o_ref, p_scr, acc_scr):
    i = pl.program_id(1)
    t = T_ATT
    qT = qT_ref[...]
    row = lax.broadcasted_iota(jnp.int32, qT.shape, 0)
    zero = jnp.zeros_like(qT)
    q_sub = (jnp.where(row < DA_HEAD_DIM, qT, zero), jnp.where(row >= DA_HEAD_DIM, qT, zero))

    def tiles(j):
        off = pl.multiple_of(jnp.minimum(j, pl.num_programs(1) - 1) * t, t)
        kt = k_ref[pl.ds(off, t), :]
        vt = vT_ref[:, pl.ds(off, t)]
        b = bias_ref[jnp.clip(i - j, -1, ND_ATT) + 1]
        return kt, vt, b

    def finish(acc1, l1, acc2, l2):
        lam = (jnp.exp(jnp.sum(lq1_ref[...] * lk1_ref[...], axis=-1, keepdims=True))
               - jnp.exp(jnp.sum(lq2_ref[...] * lk2_ref[...], axis=-1, keepdims=True)) + LAM_INIT)
        o = acc1 / l1 - lam * (acc2 / l2)
        ms = jnp.mean(o * o, axis=0, keepdims=True)
        y = o * lax.rsqrt(ms + EPS) * (subln_ref[...] * (1.0 - LAM_INIT))
        o_ref[...] = y.T.astype(BF16)

    @pl.when(bounded_ref[0] == 1)
    def _():
        def probs(j, sub, l8):
            kt, _, b = tiles(j)
            p = jnp.exp2(jnp.dot(kt, q_sub[sub], preferred_element_type=F32) + b)
            l8 = l8 + jnp.sum(p.reshape(t // SUBLANES, SUBLANES, t), axis=0)
            return p.astype(BF16), l8

        n_tiles = i + 1
        n_single = lax.rem(n_tiles, ATT_UNROLL)
        n_rounds = jnp.maximum(lax.div(n_tiles, ATT_UNROLL), 1)

        def weighted(j, sub, p):
            _, vt, _ = tiles(j)
            acc_scr[sub] += jnp.dot(vt, p, preferred_element_type=F32)

        def single(j, l8s):
            l8s = list(l8s)
            for sub in range(2):
                p, l8s[sub] = probs(j, sub, l8s[sub])
                weighted(j, sub, p)
            return tuple(l8s)

        def round_step(r, l8s, consume, produce):
            l8s = list(l8s)
            for u in range(ATT_UNROLL):
                for sub in range(2):
                    if consume:
                        weighted(n_single + ATT_UNROLL * (r - 1) + u, sub, p_scr[sub, u])
                    if produce:
                        p_scr[sub, u], l8s[sub] = probs(n_single + ATT_UNROLL * r + u, sub, l8s[sub])
            return tuple(l8s)

        acc_scr[...] = jnp.zeros(acc_scr.shape, F32)
        zeros_l = jnp.zeros((SUBLANES, t), F32)
        l8s = lax.fori_loop(0, n_single, single, (zeros_l, zeros_l))
        l8s = round_step(0, l8s, consume=False, produce=True)
        l8s = lax.fori_loop(1, n_rounds, functools.partial(round_step, consume=True, produce=True), l8s)
        round_step(n_rounds, l8s, consume=True, produce=False)
        finish(acc_scr[0], jnp.sum(l8s[0], axis=0, keepdims=True),
               acc_scr[1], jnp.sum(l8s[1], axis=0, keepdims=True))

    @pl.when(bounded_ref[0] != 1)
    def _():
        def body(j, carry):
            kt, vt, b = tiles(j)
            new = []
            for sub in range(2):
                m, l, acc = carry[sub]
                s = jnp.dot(kt, q_sub[sub], preferred_element_type=F32) + b
                m_new = jnp.maximum(m, jnp.max(s, axis=0, keepdims=True))
                alpha = jnp.exp2(m - m_new)
                p = jnp.exp2(s - m_new)
                l_new = alpha * l + jnp.sum(p, axis=0, keepdims=True)
                acc_new = alpha * acc + jnp.dot(vt, p.astype(BF16), preferred_element_type=F32)
                new.append((m_new, l_new, acc_new))
            return tuple(new)

        init1 = (jnp.full((1, t), NEG_BIG, F32), jnp.zeros((1, t), F32), jnp.zeros((DA_V_DIM, t), F32))
        (_, l1, acc1), (_, l2, acc2) = lax.fori_loop(0, i + 1, body, (init1, init1))
        finish(acc1, l1, acc2, l2)


def _diff_attn(bounded, lq1, lk1, lq2, lk2, outT, outN, bias, subln):
    s = outN.shape[1]
    t = T_ATT
    vec = pl.BlockSpec((1, DA_HEAD_DIM), lambda h, i: (0, 0))
    return pl.pallas_call(
        _attn_kernel,
        grid=(DA_HEADS, s // t),
        in_specs=[
            pl.BlockSpec(memory_space=pltpu.SMEM),
            vec, vec, vec, vec,
            pl.BlockSpec((None, DA_V_DIM, t), lambda h, i: (0, h, i)),
            pl.BlockSpec((None, s, DA_V_DIM), lambda h, i: (0, 0, h)),
            pl.BlockSpec((None, DA_V_DIM, s), lambda h, i: (1, h, 0)),
            pl.BlockSpec((N_BIAS_TILES, None, t, t), lambda h, i: (0, h, 0, 0)),
            pl.BlockSpec((DA_V_DIM, 1), lambda h, i: (0, 0)),
        ],
        out_specs=pl.BlockSpec((t, DA_V_DIM), lambda h, i: (i, h)),
        out_shape=jax.ShapeDtypeStruct((s, ATTN_WIDTH), BF16),
        scratch_shapes=[
            pltpu.VMEM((2, ATT_UNROLL, t, t), BF16),
            pltpu.VMEM((2, DA_V_DIM, t), F32),
        ],
        compiler_params=pltpu.CompilerParams(
            dimension_semantics=("arbitrary", "arbitrary"), vmem_limit_bytes=VMEM_LIMIT),
        name="diff_attn",
    )(bounded, lq1, lk1, lq2, lk2, outT, outN, outT, bias, subln)


def _rglru_kernel(xr_ref, yr_ref, cw_ref, cb_ref, wg_ref, gab_ref, gxb_ref, lam_ref, gn_ref, o_ref,
                  xpad, hstate, a_scr, b_scr, h_scr):
    step = pl.program_id(0)
    tr = TR_RNN
    c = RNN_WIDTH

    @pl.when(step == 0)
    def _():
        xpad[0:SUBLANES, :] = jnp.zeros((SUBLANES, c), F32)
        hstate[...] = jnp.zeros((1, c), F32)

    x = xr_ref[...].astype(F32)
    xpad[SUBLANES:SUBLANES + tr, :] = x
    xc = cb_ref[...] + cw_ref[CONV_WIDTH - 1:CONV_WIDTH, :] * x
    for back in range(1, CONV_WIDTH):
        w_row = cw_ref[CONV_WIDTH - 1 - back:CONV_WIDTH - back, :]
        xc = xc + w_row * xpad[SUBLANES - back:SUBLANES - back + tr, :]
    xpad[0:SUBLANES, :] = x[tr - SUBLANES:tr, :]

    xcb = xc.astype(BF16)
    ga_parts, gx_parts = [], []
    for g in range(RNN_BLOCKS):
        gg = jnp.dot(xcb[:, g * RNN_BLOCK:(g + 1) * RNN_BLOCK], wg_ref[g], preferred_element_type=F32)
        ga_parts.append(gg[:, :RNN_BLOCK])
        gx_parts.append(gg[:, RNN_BLOCK:])
    gate_a = jax.nn.sigmoid(jnp.concatenate(ga_parts, axis=-1) + gab_ref[...])
    gate_x = jax.nn.sigmoid(jnp.concatenate(gx_parts, axis=-1) + gxb_ref[...])
    z = -lam_ref[...]
    softplus = jnp.maximum(z, 0.0) + jnp.log1p(jnp.exp(-jnp.abs(z)))
    log_a = -RG_C * gate_a * softplus
    a = jnp.exp(log_a)
    one_m_a2 = 1.0 - a * a
    root = jnp.where(one_m_a2 > 0.0, one_m_a2 * lax.rsqrt(one_m_a2), 0.0)
    u = (xc * gate_x) * root

    groups = tr // SUBLANES
    a3 = a.reshape(groups, SUBLANES, c)
    b3 = u.reshape(groups, SUBLANES, c)
    row = lax.broadcasted_iota(jnp.int32, a3.shape, 1)
    shift = 1
    while shift < SUBLANES:
        a_sh = pltpu.roll(a3, shift, axis=1)
        b_sh = pltpu.roll(b3, shift, axis=1)
        keep = row >= shift
        b3 = jnp.where(keep, a3 * b_sh + b3, b3)
        a3 = jnp.where(keep, a3 * a_sh, a3)
        shift *= 2
    a_scr[...] = a3.reshape(tr, c)
    b_scr[...] = b3.reshape(tr, c)

    def carry_body(g, h):
        r0 = pl.multiple_of(g * SUBLANES, SUBLANES)
        out = a_scr[pl.ds(r0, SUBLANES), :] * h + b_scr[pl.ds(r0, SUBLANES), :]
        h_scr[pl.ds(r0, SUBLANES), :] = out
        return out[SUBLANES - 1:SUBLANES, :]

    hstate[...] = lax.fori_loop(0, groups, carry_body, hstate[...])

    y = yr_ref[...].astype(F32)
    gelu = 0.5 * y * (1.0 + jnp.tanh(math.sqrt(2.0 / math.pi) * (y + 0.044715 * (y * y * y))))
    o_ref[...] = (_rms(h_scr[...] * gelu) * gn_ref[...]).astype(BF16)


def _rglru(outN, conv_w, conv_b, w_gates, ga_b, gx_b, rg_lambda, gn):
    s = outN.shape[1]
    c = RNN_WIDTH
    row = pl.BlockSpec((1, c), lambda t: (0, 0))
    return pl.pallas_call(
        _rglru_kernel,
        grid=(s // TR_RNN,),
        in_specs=[
            pl.BlockSpec((None, TR_RNN, c), lambda t: (1, t, 0)),
            pl.BlockSpec((None, TR_RNN, c), lambda t: (2, t, 0)),
            pl.BlockSpec((CONV_WIDTH, c), lambda t: (0, 0)),
            row,
            pl.BlockSpec((RNN_BLOCKS, RNN_BLOCK, 2 * RNN_BLOCK), lambda t: (0, 0, 0)),
            row, row, row, row,
        ],
        out_specs=pl.BlockSpec((TR_RNN, c), lambda t: (t, 0)),
        out_shape=jax.ShapeDtypeStruct((s, c), BF16),
        scratch_shapes=[
            pltpu.VMEM((TR_RNN + SUBLANES, c), F32),
            pltpu.VMEM((1, c), F32),
            pltpu.VMEM((TR_RNN, c), F32),
            pltpu.VMEM((TR_RNN, c), F32),
            pltpu.VMEM((TR_RNN, c), F32),
        ],
        compiler_params=pltpu.CompilerParams(
            dimension_semantics=("arbitrary",), vmem_limit_bytes=VMEM_LIMIT),
        name="rglru",
    )(outN, outN, conv_w, conv_b, w_gates, ga_b, gx_b, rg_lambda, gn)


def _outproj_kernel(x_ref, oa_ref, or_ref, w_ref, o_ref):
    y = jnp.dot(oa_ref[...], w_ref[0:ATTN_WIDTH, :], preferred_element_type=F32)
    y = y + jnp.dot(or_ref[...], w_ref[ATTN_WIDTH:, :], preferred_element_type=F32)
    o_ref[...] = x_ref[...] + y


def _out_proj(x, oa, orn, w):
    s = x.shape[0]
    tm = TM_PROJ
    return pl.pallas_call(
        _outproj_kernel,
        grid=(s // tm,),
        in_specs=[
            pl.BlockSpec((tm, D_MODEL), lambda i: (i, 0)),
            pl.BlockSpec((tm, ATTN_WIDTH), lambda i: (i, 0)),
            pl.BlockSpec((tm, RNN_WIDTH), lambda i: (i, 0)),
            pl.BlockSpec((ATTN_WIDTH + RNN_WIDTH, D_MODEL), lambda i: (0, 0)),
        ],
        out_specs=pl.BlockSpec((tm, D_MODEL), lambda i: (i, 0)),
        out_shape=jax.ShapeDtypeStruct((s, D_MODEL), F32),
        compiler_params=pltpu.CompilerParams(
            dimension_semantics=("arbitrary",), vmem_limit_bytes=VMEM_LIMIT),
        name="out_proj",
    )(x, oa, orn, w)


def _memkv_kernel(mem_ref, g_ref, wk_ref, wv_ref, kg_ref, kT_ref, v_ref):
    m = (_rms(mem_ref[...]) * g_ref[...]).astype(BF16)
    k = jnp.dot(m, wk_ref[...], preferred_element_type=F32)
    parts = [_rms(k[:, h * X_HEAD_DIM:(h + 1) * X_HEAD_DIM]) for h in range(X_HEADS)]
    kn = jnp.concatenate(parts, axis=-1) * kg_ref[...]
    kT_ref[...] = kn.T.astype(BF16)
    v_ref[...] = jnp.dot(m, wv_ref[...], preferred_element_type=F32).astype(BF16)


def _mem_kv(mem, g, wk, wv, kg):
    return pl.pallas_call(
        _memkv_kernel,
        out_shape=[
            jax.ShapeDtypeStruct((X_WIDTH, MEM_LEN), BF16),
            jax.ShapeDtypeStruct((MEM_LEN, X_WIDTH), BF16),
        ],
        compiler_params=pltpu.CompilerParams(vmem_limit_bytes=VMEM_LIMIT),
        name="mem_kv",
    )(mem, g, wk, wv, kg)


def _xattn_kernel(x_ref, g_ref, wq_ref, qg_ref, kT_ref, v_ref, wo_ref, o_ref):
    x = x_ref[...]
    hx = (_rms(x) * g_ref[...]).astype(BF16)
    q = jnp.dot(hx, wq_ref[...], preferred_element_type=F32)
    outs = []
    for h in range(X_HEADS):
        sl = slice(h * X_HEAD_DIM, (h + 1) * X_HEAD_DIM)
        qn = (_rms(q[:, sl]) * qg_ref[:, sl]).astype(BF16)
        s = jnp.dot(qn, kT_ref[sl, :], preferred_element_type=F32)
        p = jnp.exp2(s - jnp.max(s, axis=-1, keepdims=True))
        l = jnp.sum(p, axis=-1, keepdims=True)
        o = jnp.dot(p.astype(BF16), v_ref[:, sl], preferred_element_type=F32) / l
        outs.append(o.astype(BF16))
    o_all = jnp.concatenate(outs, axis=-1)
    o_ref[...] = x + jnp.dot(o_all, wo_ref[...], preferred_element_type=F32)


def _xattn(x, g, wq, qg, kT, v, wo):
    s = x.shape[0]
    tm = TM_PROJ
    full = lambda shape: pl.BlockSpec(shape, lambda i: tuple(0 for _ in shape))
    return pl.pallas_call(
        _xattn_kernel,
        grid=(s // tm,),
        in_specs=[
            pl.BlockSpec((tm, D_MODEL), lambda i: (i, 0)),
            full((1, D_MODEL)),
            full((D_MODEL, X_WIDTH)),
            full((1, X_WIDTH)),
            full((X_WIDTH, MEM_LEN)),
            full((MEM_LEN, X_WIDTH)),
            full((X_WIDTH, D_MODEL)),
        ],
        out_specs=pl.BlockSpec((tm, D_MODEL), lambda i: (i, 0)),
        out_shape=jax.ShapeDtypeStruct((s, D_MODEL), F32),
        compiler_params=pltpu.CompilerParams(
            dimension_semantics=("arbitrary",), vmem_limit_bytes=VMEM_LIMIT),
        name="xattn",
    )(x, g, wq, qg, kT, v, wo)


def _ffn_kernel(x_ref, g_ref, wg_ref, wu_ref, wd_ref, o_ref, h_scr, acc_scr):
    f = pl.program_id(1)

    @pl.when(f == 0)
    def _():
        x = x_ref[...]
        h_scr[...] = (_rms(x) * g_ref[...]).astype(BF16)
        acc_scr[...] = x

    h = h_scr[...]
    gate = jnp.dot(h, wg_ref[...], preferred_element_type=F32)
    up = jnp.dot(h, wu_ref[...], preferred_element_type=F32)
    act = (gate * jax.nn.sigmoid(gate) * up).astype(BF16)
    acc_scr[...] += jnp.dot(act, wd_ref[...], preferred_element_type=F32)

    @pl.when(f == pl.num_programs(1) - 1)
    def _():
        o_ref[...] = acc_scr[...]


def _ffn(x, g, w_gate_up, w_down):
    s = x.shape[0]
    tm = TM_PROJ
    nf = D_FF // TF_FFN
    return pl.pallas_call(
        _ffn_kernel,
        grid=(s // tm, nf),
        in_specs=[
            pl.BlockSpec((tm, D_MODEL), lambda i, f: (i, 0)),
            pl.BlockSpec((1, D_MODEL), lambda i, f: (0, 0)),
            pl.BlockSpec((D_MODEL, TF_FFN), lambda i, f: (0, f)),
            pl.BlockSpec((D_MODEL, TF_FFN), lambda i, f: (0, nf + f)),
            pl.BlockSpec((TF_FFN, D_MODEL), lambda i, f: (f, 0)),
        ],
        out_specs=pl.BlockSpec((tm, D_MODEL), lambda i, f: (i, 0)),
        out_shape=jax.ShapeDtypeStruct((s, D_MODEL), F32),
        scratch_shapes=[pltpu.VMEM((tm, D_MODEL), BF16), pltpu.VMEM((tm, D_MODEL), F32)],
        compiler_params=pltpu.CompilerParams(
            dimension_semantics=("arbitrary", "arbitrary"), vmem_limit_bytes=VMEM_LIMIT),
        name="ffn",
    )(x, g, w_gate_up, w_gate_up, w_down)


def kernel(x, mem, rel_bias, attn_norm, w_in, da_q_norm, da_k_norm, da_lambda_q1, da_lambda_k1, da_lambda_q2, da_lambda_k2, da_subln, conv_w, conv_b, gate_a_w, gate_a_b, gate_x_w, gate_x_b, rg_lambda, rnn_out_norm, w_out, xattn_norm, mem_norm, xq_w, xk_w, xv_w, xq_norm, xk_norm, xo_w, ffn_norm, w_gate_up, w_down):
    assert x.shape == (1, SEQ, D_MODEL) and mem.shape == (1, MEM_LEN, D_MODEL)
    assert w_in.shape == (1, D_MODEL, 3 * ATTN_WIDTH + 2 * RNN_WIDTH)
    assert w_gate_up.shape == (1, D_MODEL, 2 * D_FF) and w_down.shape == (1, D_FF, D_MODEL)
    assert SEQ % T_ATT == 0 and SEQ % TM_PROJ == 0 and SEQ % TR_RNN == 0 and D_FF % TF_FFN == 0
    x2d = x[0]
    row = lambda v: v.reshape(1, -1).astype(F32)

    a = ATTN_WIDTH
    wi = w_in[0]
    w_in_b = jnp.concatenate([wi[:, 0:a], wi[:, 2 * a:3 * a], wi[:, a:2 * a], wi[:, 3 * a:]], axis=1).astype(BF16)
    grp = jnp.arange(MXU_DIM) // DA_HEAD_DIM
    gsum = jnp.where(grp[:, None] == grp[None, :], 1.0 / DA_HEAD_DIM, 0.0).astype(BF16)
    reps = ATTN_WIDTH // DA_HEAD_DIM
    qg = row(jnp.tile(da_q_norm[0], reps)) * (DA_HEAD_DIM ** -0.5 * LOG2E)
    kg = row(jnp.tile(da_k_norm[0], reps))

    outT, outN = _in_proj(x2d, row(attn_norm[0]), w_in_b, gsum, qg, kg)

    shift, bounded = _logit_bound(rel_bias, da_q_norm[0], da_k_norm[0])
    bias = _bias_tiles(rel_bias.astype(F32), shift)
    o_attn = _diff_attn(bounded, row(da_lambda_q1[0]), row(da_lambda_k1[0]), row(da_lambda_q2[0]), row(da_lambda_k2[0]),
                        outT, outN, bias, da_subln[0].reshape(DA_V_DIM, 1).astype(F32))

    w_gates = jnp.concatenate([gate_a_w[0], gate_x_w[0]], axis=-1).astype(BF16)
    o_rnn = _rglru(outN, conv_w[0].astype(F32), row(conv_b[0]), w_gates, row(gate_a_b[0]), row(gate_x_b[0]),
                   row(rg_lambda[0]), row(rnn_out_norm[0]))

    x1 = _out_proj(x2d, o_attn, o_rnn, w_out[0].astype(BF16))

    kT, vm = _mem_kv(mem[0], row(mem_norm[0]), xk_w[0].astype(BF16), xv_w[0].astype(BF16),
                     row(jnp.tile(xk_norm[0], X_HEADS)))
    xqg = row(jnp.tile(xq_norm[0], X_HEADS)) * (X_HEAD_DIM ** -0.5 * LOG2E)
    x2 = _xattn(x1, row(xattn_norm[0]), xq_w[0].astype(BF16), xqg, kT, vm, xo_w[0].astype(BF16))

    x3 = _ffn(x2, row(ffn_norm[0]), w_gate_up[0].astype(BF16), w_down[0].astype(BF16))
    return x3[None]
```

```python
import functools
import math

import jax
import jax.numpy as jnp
from jax import lax
from jax.experimental import pallas as pl
from jax.experimental.pallas import tpu as pltpu

F32 = jnp.float32
BF16 = jnp.bfloat16

D_MODEL = 2048
SEQ = 16384
CHUNK = 64
CHUNK_SHIFT = CHUNK.bit_length() - 1
assert 1 << CHUNK_SHIFT == CHUNK
MEM_LEN = 256
EPS = 1e-6
ATTN_WIDTH = 1024
DA_HEAD_DIM = 64
DA_HEADS = 8
DA_V_DIM = 2 * DA_HEAD_DIM
RNN_WIDTH = 1024
RNN_BLOCKS = 8
RNN_BLOCK = RNN_WIDTH // RNN_BLOCKS
CONV_WIDTH = 4
RG_C = 8.0
X_HEADS = 4
X_HEAD_DIM = 128
X_WIDTH = X_HEADS * X_HEAD_DIM
D_FF = 5632
NUM_BUCKETS = 32
MAX_DISTANCE = 1024
LAM_INIT = 0.8 - 0.6 * math.exp(-0.3 * 0)
LOG2E = 1.4426950408889634
NEG_BIG = -1e30
MAX_EXP2_SPREAD = 120.0

LANES = 128
SUBLANES = 8
MXU_DIM = 256
VMEM_LIMIT = 56 * 1024 * 1024

TM_PROJ = 512
TM_IN = 1024
TM_FFN = 1024
SEG = 1024
T_ATT = 512
TR_RNN = 256
TF_FFN = 512


def _far_bucket_distance():
    half = NUM_BUCKETS // 2
    max_exact = half // 2
    n = max_exact
    while max_exact + int(math.log(n / max_exact) / math.log(MAX_DISTANCE / max_exact) * (half - max_exact)) < half - 1:
        n += 1
    return n


FAR_DIST = _far_bucket_distance()
ND_ATT = next(d for d in range(1, 64) if (d - 1) * T_ATT + 1 >= FAR_DIST + 64)
FAR_BUCKET = NUM_BUCKETS // 2 - 1
N_BIAS_TILES = ND_ATT + 2
ATT_UNROLL = 4


def _rms(x, eps=EPS):
    return x * lax.rsqrt(jnp.mean(x * x, axis=-1, keepdims=True) + eps)


def _inproj_kernel(x_ref, g_ref, w_ref, gsum_ref, qg_ref, kg_ref, outT_ref, outN_ref, h_scr):
    j = pl.program_id(1)

    @pl.when(j == 0)
    def _():
        h_scr[...] = (_rms(x_ref[...]) * g_ref[...]).astype(BF16)

    acc = jnp.dot(h_scr[...], w_ref[...], preferred_element_type=F32)

    def group_norm(a, gain_row):
        sq = (a * a).astype(BF16)
        parts = [jnp.dot(sq[:, c * MXU_DIM:(c + 1) * MXU_DIM], gsum_ref[...], preferred_element_type=F32)
                 for c in range(SEG // MXU_DIM)]
        ms = jnp.concatenate(parts, axis=-1)
        return a * lax.rsqrt(ms + EPS) * gain_row

    @pl.when(j == 0)
    def _():
        outT_ref[...] = group_norm(acc, qg_ref[...]).T.astype(BF16)

    @pl.when(j == 1)
    def _():
        outT_ref[...] = acc.T.astype(BF16)

    @pl.when(j == 2)
    def _():
        outN_ref[...] = group_norm(acc, kg_ref[...]).astype(BF16)

    @pl.when(j >= 3)
    def _():
        outN_ref[...] = acc.astype(BF16)


def _in_proj(x, g, w, gsum, qg, kg):
    s = x.shape[0]
    nseg = w.shape[1] // SEG
    return pl.pallas_call(
        _inproj_kernel,
        grid=(s // TM_IN, nseg),
        in_specs=[
            pl.BlockSpec((TM_IN, D_MODEL), lambda i, j: (i, 0)),
            pl.BlockSpec((1, D_MODEL), lambda i, j: (0, 0)),
            pl.BlockSpec((D_MODEL, SEG), lambda i, j: (0, j)),
            pl.BlockSpec((MXU_DIM, MXU_DIM), lambda i, j: (0, 0)),
            pl.BlockSpec((1, SEG), lambda i, j: (0, 0)),
            pl.BlockSpec((1, SEG), lambda i, j: (0, 0)),
        ],
        out_specs=[
            pl.BlockSpec((None, SEG, TM_IN), lambda i, j: (jnp.minimum(j, 1), 0, i)),
            pl.BlockSpec((None, TM_IN, SEG), lambda i, j: (jnp.maximum(j - 2, 0), i, 0)),
        ],
        out_shape=[
            jax.ShapeDtypeStruct((2, SEG, s), BF16),
            jax.ShapeDtypeStruct((3, s, SEG), BF16),
        ],
        scratch_shapes=[pltpu.VMEM((TM_IN, D_MODEL), BF16)],
        compiler_params=pltpu.CompilerParams(
            dimension_semantics=("arbitrary", "arbitrary"), vmem_limit_bytes=VMEM_LIMIT),
        name="in_proj",
    )(x, g, w, gsum, qg, kg)


def _bias_kernel(rb_ref, shift_ref, out_ref, bucket_scr):
    d = pl.program_id(0) - 1
    t = T_ATT
    half = NUM_BUCKETS // 2

    def fill(n_buckets):
        def head(h, carry):
            bucket = bucket_scr[...]
            val = jnp.zeros((t, t), F32)
            for b in range(n_buckets):
                val = jnp.where(bucket == b, rb_ref[b, h], val)
            val = (val - rb_ref[FAR_BUCKET, h]) * LOG2E - shift_ref[h]
            out_ref[h] = jnp.where(bucket >= 0, val, NEG_BIG)
            return carry
        lax.fori_loop(0, DA_HEADS, head, 0)

    @pl.when(jnp.logical_and(d >= 0, d < ND_ATT))
    def _():
        kk = lax.broadcasted_iota(jnp.int32, (t, t), 0)
        qq = lax.broadcasted_iota(jnp.int32, (t, t), 1)
        rel = kk - qq - d * t
        max_exact = half // 2
        n = jnp.abs(rel)
        large = max_exact + (jnp.log(jnp.maximum(n, 1).astype(F32) / max_exact)
                             / math.log(MAX_DISTANCE / max_exact) * (half - max_exact)).astype(jnp.int32)
        large = jnp.minimum(large, half - 1)
        bucket = jnp.where(rel > 0, half, 0) + jnp.where(n < max_exact, n, large)
        allowed = (kk >> CHUNK_SHIFT) <= ((qq + d * t) >> CHUNK_SHIFT)
        bucket_scr[...] = jnp.where(allowed, bucket, -1)

    @pl.when(d == 0)
    def _():
        fill(NUM_BUCKETS)

    @pl.when(jnp.logical_and(d >= 1, d < ND_ATT))
    def _():
        fill(half)

    @pl.when(jnp.logical_or(d < 0, d >= ND_ATT))
    def _():
        def head(h, carry):
            out_ref[h] = jnp.full((t, t), jnp.where(d < 0, NEG_BIG, -shift_ref[h]), F32)
            return carry
        lax.fori_loop(0, DA_HEADS, head, 0)


def _bias_tiles(rel_bias, shift):
    return pl.pallas_call(
        _bias_kernel,
        grid=(N_BIAS_TILES,),
        in_specs=[pl.BlockSpec(memory_space=pltpu.SMEM), pl.BlockSpec(memory_space=pltpu.SMEM)],
        out_specs=pl.BlockSpec((None, DA_HEADS, T_ATT, T_ATT), lambda d: (d, 0, 0, 0)),
        out_shape=jax.ShapeDtypeStruct((N_BIAS_TILES, DA_HEADS, T_ATT, T_ATT), F32),
        scratch_shapes=[pltpu.VMEM((T_ATT, T_ATT), jnp.int32)],
        compiler_params=pltpu.CompilerParams(
            dimension_semantics=("arbitrary",), vmem_limit_bytes=VMEM_LIMIT),
        name="bias_tiles",
    )(rel_bias, shift)


def _logit_bound(rel_bias, q_gain, k_gain):
    rb = rel_bias.astype(F32)
    b2 = (rb - rb[FAR_BUCKET]) * LOG2E
    qk = (LOG2E * DA_HEAD_DIM ** 0.5 * (1.0 + 2.0 ** -6)) * jnp.max(jnp.abs(q_gain)) * jnp.max(jnp.abs(k_gain))
    bound = qk + jnp.max(b2, axis=0)
    spread = 2.0 * qk + jnp.max(jnp.max(b2, axis=0) - jnp.min(b2, axis=0))
    bounded = spread <= MAX_EXP2_SPREAD
    return jnp.where(bounded, bound, 0.0).astype(F32), bounded.astype(jnp.int32).reshape(1)


def _attn_kernel(bounded_ref, lq1_ref, lk1_ref, lq2_ref, lk2_ref, qT_ref, k_ref, vT_ref, bias_ref, subln_ref,
                 o_ref, p_scr, acc_scr):
    i = pl.program_id(1)
    t = T_ATT
    qT = qT_ref[...]
    row = lax.broadcasted_iota(jnp.int32, qT.shape, 0)
    zero = jnp.zeros_like(qT)
    q_sub = (jnp.where(row < DA_HEAD_DIM, qT, zero), jnp.where(row >= DA_HEAD_DIM, qT, zero))

    def tiles(j):
        off = pl.multiple_of(jnp.minimum(j, pl.num_programs(1) - 1) * t, t)
        kt = k_ref[pl.ds(off, t), :]
        vt = vT_ref[:, pl.ds(off, t)]
        b = bias_ref[jnp.clip(i - j, -1, ND_ATT) + 1]
        return kt, vt, b

    def finish(acc1, l1, acc2, l2):
        lam = (jnp.exp(jnp.sum(lq1_ref[...] * lk1_ref[...], axis=-1, keepdims=True))
               - jnp.exp(jnp.sum(lq2_ref[...] * lk2_ref[...], axis=-1, keepdims=True)) + LAM_INIT)
        o = acc1 / l1 - lam * (acc2 / l2)
        ms = jnp.mean(o * o, axis=0, keepdims=True)
        y = o * lax.rsqrt(ms + EPS) * (subln_ref[...] * (1.0 - LAM_INIT))
        o_ref[...] = y.T.astype(BF16)

    @pl.when(bounded_ref[0] == 1)
    def _():
        def probs(j, sub, l8):
            kt, _, b = tiles(j)
            p = jnp.exp2(jnp.dot(kt, q_sub[sub], preferred_element_type=F32) + b)
            l8 = l8 + jnp.sum(p.reshape(t // SUBLANES, SUBLANES, t), axis=0)
            return p.astype(BF16), l8

        n_tiles = i + 1
        n_single = lax.rem(n_tiles, ATT_UNROLL)
        n_rounds = jnp.maximum(lax.div(n_tiles, ATT_UNROLL), 1)

        def weighted(j, sub, p):
            _, vt, _ = tiles(j)
            acc_scr[sub] += jnp.dot(vt, p, preferred_element_type=F32)

        def single(j, l8s):
            l8s = list(l8s)
            for sub in range(2):
                p, l8s[sub] = probs(j, sub, l8s[sub])
                weighted(j, sub, p)
            return tuple(l8s)

        def round_step(r, l8s, consume, produce):
            l8s = list(l8s)
            for u in range(ATT_UNROLL):
                for sub in range(2):
                    if consume:
                        weighted(n_single + ATT_UNROLL * (r - 1) + u, sub, p_scr[sub, u])
                    if produce:
                        p_scr[sub, u], l8s[sub] = probs(n_single + ATT_UNROLL * r + u, sub, l8s[sub])
            return tuple(l8s)

        acc_scr[...] = jnp.zeros(acc_scr.shape, F32)
        zeros_l = jnp.zeros((SUBLANES, t), F32)
        l8s = lax.fori_loop(0, n_single, single, (zeros_l, zeros_l))
        l8s = round_step(0, l8s, consume=False, produce=True)
        l8s = lax.fori_loop(1, n_rounds, functools.partial(round_step, consume=True, produce=True), l8s)
        round_step(n_rounds, l8s, consume=True, produce=False)
        finish(acc_scr[0], jnp.sum(l8s[0], axis=0, keepdims=True),
               acc_scr[1], jnp.sum(l8s[1], axis=0, keepdims=True))

    @pl.when(bounded_ref[0] != 1)
    def _():
        def body(j, carry):
            kt, vt, b = tiles(j)
            new = []
            for sub in range(2):
                m, l, acc = carry[sub]
                s = jnp.dot(kt, q_sub[sub], preferred_element_type=F32) + b
                m_new = jnp.maximum(m, jnp.max(s, axis=0, keepdims=True))
                alpha = jnp.exp2(m - m_new)
                p = jnp.exp2(s - m_new)
                l_new = alpha * l + jnp.sum(p, axis=0, keepdims=True)
                acc_new = alpha * acc + jnp.dot(vt, p.astype(BF16), preferred_element_type=F32)
                new.append((m_new, l_new, acc_new))
            return tuple(new)

        init1 = (jnp.full((1, t), NEG_BIG, F32), jnp.zeros((1, t), F32), jnp.zeros((DA_V_DIM, t), F32))
        (_, l1, acc1), (_, l2, acc2) = lax.fori_loop(0, i + 1, body, (init1, init1))
        finish(acc1, l1, acc2, l2)


def _diff_attn(bounded, lq1, lk1, lq2, lk2, outT, outN, bias, subln):
    s = outN.shape[1]
    t = T_ATT
    vec = pl.BlockSpec((1, DA_HEAD_DIM), lambda h, i: (0, 0))
    return pl.pallas_call(
        _attn_kernel,
        grid=(DA_HEADS, s // t),
        in_specs=[
            pl.BlockSpec(memory_space=pltpu.SMEM),
            vec, vec, vec, vec,
            pl.BlockSpec((None, DA_V_DIM, t), lambda h, i: (0, h, i)),
            pl.BlockSpec((None, s, DA_V_DIM), lambda h, i: (0, 0, h)),
            pl.BlockSpec((None, DA_V_DIM, s), lambda h, i: (1, h, 0)),
            pl.BlockSpec((N_BIAS_TILES, None, t, t), lambda h, i: (0, h, 0, 0)),
            pl.BlockSpec((DA_V_DIM, 1), lambda h, i: (0, 0)),
        ],
        out_specs=pl.BlockSpec((t, DA_V_DIM), lambda h, i: (i, h)),
        out_shape=jax.ShapeDtypeStruct((s, ATTN_WIDTH), BF16),
        scratch_shapes=[
            pltpu.VMEM((2, ATT_UNROLL, t, t), BF16),
            pltpu.VMEM((2, DA_V_DIM, t), F32),
        ],
        compiler_params=pltpu.CompilerParams(
            dimension_semantics=("arbitrary", "arbitrary"), vmem_limit_bytes=VMEM_LIMIT),
        name="diff_attn",
    )(bounded, lq1, lk1, lq2, lk2, outT, outN, outT, bias, subln)


def _rglru_kernel(xr_ref, yr_ref, cw_ref, cb_ref, wg_ref, gab_ref, gxb_ref, lam_ref, gn_ref, o_ref,
                  xpad, hstate, a_scr, b_scr, h_scr):
    step = pl.program_id(0)
    tr = TR_RNN
    c = RNN_WIDTH

    @pl.when(step == 0)
    def _():
        xpad[0:SUBLANES, :] = jnp.zeros((SUBLANES, c), F32)
        hstate[...] = jnp.zeros((1, c), F32)

    x = xr_ref[...].astype(F32)
    xpad[SUBLANES:SUBLANES + tr, :] = x
    xc = cb_ref[...] + cw_ref[CONV_WIDTH - 1:CONV_WIDTH, :] * x
    for back in range(1, CONV_WIDTH):
        w_row = cw_ref[CONV_WIDTH - 1 - back:CONV_WIDTH - back, :]
        xc = xc + w_row * xpad[SUBLANES - back:SUBLANES - back + tr, :]
    xpad[0:SUBLANES, :] = x[tr - SUBLANES:tr, :]

    xcb = xc.astype(BF16)
    ga_parts, gx_parts = [], []
    for g in range(RNN_BLOCKS):
        gg = jnp.dot(xcb[:, g * RNN_BLOCK:(g + 1) * RNN_BLOCK], wg_ref[g], preferred_element_type=F32)
        ga_parts.append(gg[:, :RNN_BLOCK])
        gx_parts.append(gg[:, RNN_BLOCK:])
    gate_a = jax.nn.sigmoid(jnp.concatenate(ga_parts, axis=-1) + gab_ref[...])
    gate_x = jax.nn.sigmoid(jnp.concatenate(gx_parts, axis=-1) + gxb_ref[...])
    z = -lam_ref[...]
    softplus = jnp.maximum(z, 0.0) + jnp.log1p(jnp.exp(-jnp.abs(z)))
    log_a = -RG_C * gate_a * softplus
    a = jnp.exp(log_a)
    one_m_a2 = 1.0 - a * a
    root = jnp.where(one_m_a2 > 0.0, one_m_a2 * lax.rsqrt(one_m_a2), 0.0)
    u = (xc * gate_x) * root

    groups = tr // SUBLANES
    a3 = a.reshape(groups, SUBLANES, c)
    b3 = u.reshape(groups, SUBLANES, c)
    row = lax.broadcasted_iota(jnp.int32, a3.shape, 1)
    shift = 1
    while shift < SUBLANES:
        a_sh = pltpu.roll(a3, shift, axis=1)
        b_sh = pltpu.roll(b3, shift, axis=1)
        keep = row >= shift
        b3 = jnp.where(keep, a3 * b_sh + b3, b3)
        a3 = jnp.where(keep, a3 * a_sh, a3)
        shift *= 2
    a_scr[...] = a3.reshape(tr, c)
    b_scr[...] = b3.reshape(tr, c)

    def carry_body(g, h):
        r0 = pl.multiple_of(g * SUBLANES, SUBLANES)
        out = a_scr[pl.ds(r0, SUBLANES), :] * h + b_scr[pl.ds(r0, SUBLANES), :]
        h_scr[pl.ds(r0, SUBLANES), :] = out
        return out[SUBLANES - 1:SUBLANES, :]

    hstate[...] = lax.fori_loop(0, groups, carry_body, hstate[...])

    y = yr_ref[...].astype(F32)
    gelu = 0.5 * y * (1.0 + jnp.tanh(math.sqrt(2.0 / math.pi) * (y + 0.044715 * (y * y * y))))
    o_ref[...] = (_rms(h_scr[...] * gelu) * gn_ref[...]).astype(BF16)


def _rglru(outN, conv_w, conv_b, w_gates, ga_b, gx_b, rg_lambda, gn):
    s = outN.shape[1]
    c = RNN_WIDTH
    row = pl.BlockSpec((1, c), lambda t: (0, 0))
    return pl.pallas_call(
        _rglru_kernel,
        grid=(s // TR_RNN,),
        in_specs=[
            pl.BlockSpec((None, TR_RNN, c), lambda t: (1, t, 0)),
            pl.BlockSpec((None, TR_RNN, c), lambda t: (2, t, 0)),
            pl.BlockSpec((CONV_WIDTH, c), lambda t: (0, 0)),
            row,
            pl.BlockSpec((RNN_BLOCKS, RNN_BLOCK, 2 * RNN_BLOCK), lambda t: (0, 0, 0)),
            row, row, row, row,
        ],
        out_specs=pl.BlockSpec((TR_RNN, c), lambda t: (t, 0)),
        out_shape=jax.ShapeDtypeStruct((s, c), BF16),
        scratch_shapes=[
            pltpu.VMEM((TR_RNN + SUBLANES, c), F32),
            pltpu.VMEM((1, c), F32),
            pltpu.VMEM((TR_RNN, c), F32),
            pltpu.VMEM((TR_RNN, c), F32),
            pltpu.VMEM((TR_RNN, c), F32),
        ],
        compiler_params=pltpu.CompilerParams(
            dimension_semantics=("arbitrary",), vmem_limit_bytes=VMEM_LIMIT),
        name="rglru",
    )(outN, outN, conv_w, conv_b, w_gates, ga_b, gx_b, rg_lambda, gn)


def _outproj_kernel(x_ref, oa_ref, or_ref, w_ref, o_ref):
    y = jnp.dot(oa_ref[...], w_ref[0:ATTN_WIDTH, :], preferred_element_type=F32)
    y = y + jnp.dot(or_ref[...], w_ref[ATTN_WIDTH:, :], preferred_element_type=F32)
    o_ref[...] = x_ref[...] + y


def _out_proj(x, oa, orn, w):
    s = x.shape[0]
    tm = TM_PROJ
    return pl.pallas_call(
        _outproj_kernel,
        grid=(s // tm,),
        in_specs=[
            pl.BlockSpec((tm, D_MODEL), lambda i: (i, 0)),
            pl.BlockSpec((tm, ATTN_WIDTH), lambda i: (i, 0)),
            pl.BlockSpec((tm, RNN_WIDTH), lambda i: (i, 0)),
            pl.BlockSpec((ATTN_WIDTH + RNN_WIDTH, D_MODEL), lambda i: (0, 0)),
        ],
        out_specs=pl.BlockSpec((tm, D_MODEL), lambda i: (i, 0)),
        out_shape=jax.ShapeDtypeStruct((s, D_MODEL), F32),
        compiler_params=pltpu.CompilerParams(
            dimension_semantics=("arbitrary",), vmem_limit_bytes=VMEM_LIMIT),
        name="out_proj",
    )(x, oa, orn, w)


def _memkv_kernel(mem_ref, g_ref, wk_ref, wv_ref, kg_ref, kT_ref, v_ref):
    m = (_rms(mem_ref[...]) * g_ref[...]).astype(BF16)
    k = jnp.dot(m, wk_ref[...], preferred_element_type=F32)
    parts = [_rms(k[:, h * X_HEAD_DIM:(h + 1) * X_HEAD_DIM]) for h in range(X_HEADS)]
    kn = jnp.concatenate(parts, axis=-1) * kg_ref[...]
    kT_ref[...] = kn.T.astype(BF16)
    v_ref[...] = jnp.dot(m, wv_ref[...], preferred_element_type=F32).astype(BF16)


def _mem_kv(mem, g, wk, wv, kg):
    return pl.pallas_call(
        _memkv_kernel,
        out_shape=[
            jax.ShapeDtypeStruct((X_WIDTH, MEM_LEN), BF16),
            jax.ShapeDtypeStruct((MEM_LEN, X_WIDTH), BF16),
        ],
        compiler_params=pltpu.CompilerParams(vmem_limit_bytes=VMEM_LIMIT),
        name="mem_kv",
    )(mem, g, wk, wv, kg)


def _xattn_kernel(x_ref, g_ref, wq_ref, qg_ref, kT_ref, v_ref, wo_ref, o_ref):
    x = x_ref[...]
    hx = (_rms(x) * g_ref[...]).astype(BF16)
    q = jnp.dot(hx, wq_ref[...], preferred_element_type=F32)
    outs = []
    for h in range(X_HEADS):
        sl = slice(h * X_HEAD_DIM, (h + 1) * X_HEAD_DIM)
        qn = (_rms(q[:, sl]) * qg_ref[:, sl]).astype(BF16)
        s = jnp.dot(qn, kT_ref[sl, :], preferred_element_type=F32)
        p = jnp.exp2(s - jnp.max(s, axis=-1, keepdims=True))
        l = jnp.sum(p, axis=-1, keepdims=True)
        o = jnp.dot(p.astype(BF16), v_ref[:, sl], preferred_element_type=F32) / l
        outs.append(o.astype(BF16))
    o_all = jnp.concatenate(outs, axis=-1)
    o_ref[...] = x + jnp.dot(o_all, wo_ref[...], preferred_element_type=F32)


def _xattn(x, g, wq, qg, kT, v, wo):
    s = x.shape[0]
    tm = TM_PROJ
    full = lambda shape: pl.BlockSpec(shape, lambda i: tuple(0 for _ in shape))
    return pl.pallas_call(
        _xattn_kernel,
        grid=(s // tm,),
        in_specs=[
            pl.BlockSpec((tm, D_MODEL), lambda i: (i, 0)),
            full((1, D_MODEL)),
            full((D_MODEL, X_WIDTH)),
            full((1, X_WIDTH)),
            full((X_WIDTH, MEM_LEN)),
            full((MEM_LEN, X_WIDTH)),
            full((X_WIDTH, D_MODEL)),
        ],
        out_specs=pl.BlockSpec((tm, D_MODEL), lambda i: (i, 0)),
        out_shape=jax.ShapeDtypeStruct((s, D_MODEL), F32),
        compiler_params=pltpu.CompilerParams(
            dimension_semantics=("arbitrary",), vmem_limit_bytes=VMEM_LIMIT),
        name="xattn",
    )(x, g, wq, qg, kT, v, wo)


def _ffn_kernel(x_ref, g_ref, wg_ref, wu_ref, wd_ref, o_ref, h_scr):
    f = pl.program_id(1)

    @pl.when(f == 0)
    def _():
        x = x_ref[...]
        h_scr[...] = (_rms(x) * g_ref[...]).astype(BF16)
        o_ref[...] = x

    h = h_scr[...]
    gate = jnp.dot(h, wg_ref[...], preferred_element_type=F32)
    up = jnp.dot(h, wu_ref[...], preferred_element_type=F32)
    act = (gate * jax.nn.sigmoid(gate) * up).astype(BF16)
    o_ref[...] += jnp.dot(act, wd_ref[...], preferred_element_type=F32)


def _ffn(x, g, w_gate_up, w_down):
    s = x.shape[0]
    tm = TM_FFN
    nf = D_FF // TF_FFN
    return pl.pallas_call(
        _ffn_kernel,
        grid=(s // tm, nf),
        in_specs=[
            pl.BlockSpec((tm, D_MODEL), lambda i, f: (i, 0)),
            pl.BlockSpec((1, D_MODEL), lambda i, f: (0, 0)),
            pl.BlockSpec((D_MODEL, TF_FFN), lambda i, f: (0, f)),
            pl.BlockSpec((D_MODEL, TF_FFN), lambda i, f: (0, nf + f)),
            pl.BlockSpec((TF_FFN, D_MODEL), lambda i, f: (f, 0)),
        ],
        out_specs=pl.BlockSpec((tm, D_MODEL), lambda i, f: (i, 0)),
        out_shape=jax.ShapeDtypeStruct((s, D_MODEL), F32),
        scratch_shapes=[pltpu.VMEM((tm, D_MODEL), BF16)],
        compiler_params=pltpu.CompilerParams(
            dimension_semantics=("arbitrary", "arbitrary"), vmem_limit_bytes=VMEM_LIMIT),
        name="ffn",
    )(x, g, w_gate_up, w_gate_up, w_down)


def kernel(x, mem, rel_bias, attn_norm, w_in, da_q_norm, da_k_norm, da_lambda_q1, da_lambda_k1, da_lambda_q2, da_lambda_k2, da_subln, conv_w, conv_b, gate_a_w, gate_a_b, gate_x_w, gate_x_b, rg_lambda, rnn_out_norm, w_out, xattn_norm, mem_norm, xq_w, xk_w, xv_w, xq_norm, xk_norm, xo_w, ffn_norm, w_gate_up, w_down):
    assert x.shape == (1, SEQ, D_MODEL) and mem.shape == (1, MEM_LEN, D_MODEL)
    assert w_in.shape == (1, D_MODEL, 3 * ATTN_WIDTH + 2 * RNN_WIDTH)
    assert w_gate_up.shape == (1, D_MODEL, 2 * D_FF) and w_down.shape == (1, D_FF, D_MODEL)
    assert SEQ % T_ATT == 0 and SEQ % TM_PROJ == 0 and SEQ % TM_IN == 0 and SEQ % TM_FFN == 0
    assert SEQ % TR_RNN == 0 and D_FF % TF_FFN == 0
    x2d = x[0]
    row = lambda v: v.reshape(1, -1).astype(F32)

    a = ATTN_WIDTH
    wi = w_in[0]
    w_in_b = jnp.concatenate([wi[:, 0:a], wi[:, 2 * a:3 * a], wi[:, a:2 * a], wi[:, 3 * a:]], axis=1).astype(BF16)
    grp = jnp.arange(MXU_DIM) // DA_HEAD_DIM
    gsum = jnp.where(grp[:, None] == grp[None, :], 1.0 / DA_HEAD_DIM, 0.0).astype(BF16)
    reps = ATTN_WIDTH // DA_HEAD_DIM
    qg = row(jnp.tile(da_q_norm[0], reps)) * (DA_HEAD_DIM ** -0.5 * LOG2E)
    kg = row(jnp.tile(da_k_norm[0], reps))

    outT, outN = _in_proj(x2d, row(attn_norm[0]), w_in_b, gsum, qg, kg)

    shift, bounded = _logit_bound(rel_bias, da_q_norm[0], da_k_norm[0])
    bias = _bias_tiles(rel_bias.astype(F32), shift)
    o_attn = _diff_attn(bounded, row(da_lambda_q1[0]), row(da_lambda_k1[0]), row(da_lambda_q2[0]), row(da_lambda_k2[0]),
                        outT, outN, bias, da_subln[0].reshape(DA_V_DIM, 1).astype(F32))

    w_gates = jnp.concatenate([gate_a_w[0], gate_x_w[0]], axis=-1).astype(BF16)
    o_rnn = _rglru(outN, conv_w[0].astype(F32), row(conv_b[0]), w_gates, row(gate_a_b[0]), row(gate_x_b[0]),
                   row(rg_lambda[0]), row(rnn_out_norm[0]))

    x1 = _out_proj(x2d, o_attn, o_rnn, w_out[0].astype(BF16))

    kT, vm = _mem_kv(mem[0], row(mem_norm[0]), xk_w[0].astype(BF16), xv_w[0].astype(BF16),
                     row(jnp.tile(xk_norm[0], X_HEADS)))
    xqg = row(jnp.tile(xq_norm[0], X_HEADS)) * (X_HEAD_DIM ** -0.5 * LOG2E)
    x2 = _xattn(x1, row(xattn_norm[0]), xq_w[0].astype(BF16), xqg, kT, vm, xo_w[0].astype(BF16))

    x3 = _ffn(x2, row(ffn_norm[0]), w_gate_up[0].astype(BF16), w_down[0].astype(BF16))
    return x3[None]
```

```python
import functools
import math

import jax
import jax.numpy as jnp
from jax import lax
from jax.experimental import pallas as pl
from jax.experimental.pallas import tpu as pltpu

F32 = jnp.float32
BF16 = jnp.bfloat16

D_MODEL = 2048
SEQ = 16384
CHUNK = 64
CHUNK_SHIFT = CHUNK.bit_length() - 1
assert 1 << CHUNK_SHIFT == CHUNK
MEM_LEN = 256
EPS = 1e-6
ATTN_WIDTH = 1024
DA_HEAD_DIM = 64
DA_HEADS = 8
DA_V_DIM = 2 * DA_HEAD_DIM
RNN_WIDTH = 1024
RNN_BLOCKS = 8
RNN_BLOCK = RNN_WIDTH // RNN_BLOCKS
CONV_WIDTH = 4
RG_C = 8.0
X_HEADS = 4
X_HEAD_DIM = 128
X_WIDTH = X_HEADS * X_HEAD_DIM
D_FF = 5632
NUM_BUCKETS = 32
MAX_DISTANCE = 1024
LAM_INIT = 0.8 - 0.6 * math.exp(-0.3 * 0)
LOG2E = 1.4426950408889634
NEG_BIG = -1e30
MAX_EXP2_SPREAD = 120.0

LANES = 128
SUBLANES = 8
MXU_DIM = 256
VMEM_LIMIT = 56 * 1024 * 1024

TM_PROJ = 512
TM_IN = 1024
TM_FFN = 1024
SEG = 1024
T_ATT = 512
TR_RNN = 256
TF_FFN = 512


def _far_bucket_distance():
    half = NUM_BUCKETS // 2
    max_exact = half // 2
    n = max_exact
    while max_exact + int(math.log(n / max_exact) / math.log(MAX_DISTANCE / max_exact) * (half - max_exact)) < half - 1:
        n += 1
    return n


FAR_DIST = _far_bucket_distance()
ND_ATT = next(d for d in range(1, 64) if (d - 1) * T_ATT + 1 >= FAR_DIST + 64)
FAR_BUCKET = NUM_BUCKETS // 2 - 1
N_BIAS_TILES = ND_ATT + 2
ATT_UNROLL = 2
Q_BLOCKS = 2
assert Q_BLOCKS == ATT_UNROLL


def _rms(x, eps=EPS):
    return x * lax.rsqrt(jnp.mean(x * x, axis=-1, keepdims=True) + eps)


def _inproj_kernel(x_ref, w_ref, gsum_ref, g_ref, qg_ref, kg_ref, outT_ref, outN_ref, h_scr):
    j = pl.program_id(1)

    @pl.when(j == 0)
    def _():
        h_scr[...] = (_rms(x_ref[...]) * g_ref[...]).astype(BF16)

    acc = jnp.dot(h_scr[...], w_ref[...], preferred_element_type=F32)

    def group_norm(a, gain_row):
        sq = (a * a).astype(BF16)
        parts = [jnp.dot(sq[:, c * MXU_DIM:(c + 1) * MXU_DIM], gsum_ref[...], preferred_element_type=F32)
                 for c in range(SEG // MXU_DIM)]
        ms = jnp.concatenate(parts, axis=-1)
        return a * lax.rsqrt(ms + EPS) * gain_row

    @pl.when(j == 0)
    def _():
        outT_ref[...] = group_norm(acc, qg_ref[...]).T.astype(BF16)

    @pl.when(j == 1)
    def _():
        outT_ref[...] = acc.T.astype(BF16)

    @pl.when(j == 2)
    def _():
        outN_ref[...] = group_norm(acc, kg_ref[...]).astype(BF16)

    @pl.when(j >= 3)
    def _():
        outN_ref[...] = acc.astype(BF16)


def _in_proj(x, g, w, gsum, qg, kg):
    s = x.shape[0]
    nseg = w.shape[1] // SEG
    return pl.pallas_call(
        _inproj_kernel,
        grid=(s // TM_IN, nseg),
        in_specs=[
            pl.BlockSpec((TM_IN, D_MODEL), lambda i, j: (i, 0)),
            pl.BlockSpec((D_MODEL, SEG), lambda i, j: (0, j)),
            pl.BlockSpec((MXU_DIM, MXU_DIM), lambda i, j: (0, 0)),
            pl.BlockSpec((1, D_MODEL), lambda i, j: (0, 0)),
            pl.BlockSpec((1, SEG), lambda i, j: (0, 0)),
            pl.BlockSpec((1, SEG), lambda i, j: (0, 0)),
        ],
        out_specs=[
            pl.BlockSpec((None, SEG, TM_IN), lambda i, j: (jnp.minimum(j, 1), 0, i)),
            pl.BlockSpec((None, TM_IN, SEG), lambda i, j: (jnp.maximum(j - 2, 0), i, 0)),
        ],
        out_shape=[
            jax.ShapeDtypeStruct((2, SEG, s), BF16),
            jax.ShapeDtypeStruct((3, s, SEG), BF16),
        ],
        scratch_shapes=[pltpu.VMEM((TM_IN, D_MODEL), BF16)],
        compiler_params=pltpu.CompilerParams(
            dimension_semantics=("arbitrary", "arbitrary"), vmem_limit_bytes=VMEM_LIMIT),
        name="in_proj",
    )(x, w, gsum, g, qg, kg)


def _bias_kernel(rb_ref, shift_ref, out_ref, bucket_scr):
    d = pl.program_id(0) - 1
    t = T_ATT
    half = NUM_BUCKETS // 2

    def fill(n_buckets):
        def head(h, carry):
            bucket = bucket_scr[...]
            val = jnp.zeros((t, t), F32)
            for b in range(n_buckets):
                val = jnp.where(bucket == b, rb_ref[b, h], val)
            val = (val - rb_ref[FAR_BUCKET, h]) * LOG2E - shift_ref[h]
            out_ref[h] = jnp.where(bucket >= 0, val, NEG_BIG)
            return carry
        lax.fori_loop(0, DA_HEADS, head, 0)

    @pl.when(jnp.logical_and(d >= 0, d < ND_ATT))
    def _():
        kk = lax.broadcasted_iota(jnp.int32, (t, t), 0)
        qq = lax.broadcasted_iota(jnp.int32, (t, t), 1)
        rel = kk - qq - d * t
        max_exact = half // 2
        n = jnp.abs(rel)
        large = max_exact + (jnp.log(jnp.maximum(n, 1).astype(F32) / max_exact)
                             / math.log(MAX_DISTANCE / max_exact) * (half - max_exact)).astype(jnp.int32)
        large = jnp.minimum(large, half - 1)
        bucket = jnp.where(rel > 0, half, 0) + jnp.where(n < max_exact, n, large)
        allowed = (kk >> CHUNK_SHIFT) <= ((qq + d * t) >> CHUNK_SHIFT)
        bucket_scr[...] = jnp.where(allowed, bucket, -1)

    @pl.when(d == 0)
    def _():
        fill(NUM_BUCKETS)

    @pl.when(jnp.logical_and(d >= 1, d < ND_ATT))
    def _():
        fill(half)

    @pl.when(jnp.logical_or(d < 0, d >= ND_ATT))
    def _():
        def head(h, carry):
            out_ref[h] = jnp.full((t, t), jnp.where(d < 0, NEG_BIG, -shift_ref[h]), F32)
            return carry
        lax.fori_loop(0, DA_HEADS, head, 0)


def _bias_tiles(rel_bias, shift):
    return pl.pallas_call(
        _bias_kernel,
        grid=(N_BIAS_TILES,),
        in_specs=[pl.BlockSpec(memory_space=pltpu.SMEM), pl.BlockSpec(memory_space=pltpu.SMEM)],
        out_specs=pl.BlockSpec((None, DA_HEADS, T_ATT, T_ATT), lambda d: (d, 0, 0, 0)),
        out_shape=jax.ShapeDtypeStruct((N_BIAS_TILES, DA_HEADS, T_ATT, T_ATT), F32),
        scratch_shapes=[pltpu.VMEM((T_ATT, T_ATT), jnp.int32)],
        compiler_params=pltpu.CompilerParams(
            dimension_semantics=("arbitrary",), vmem_limit_bytes=VMEM_LIMIT),
        name="bias_tiles",
    )(rel_bias, shift)


def _logit_bound(rel_bias, q_gain, k_gain):
    rb = rel_bias.astype(F32)
    b2 = (rb - rb[FAR_BUCKET]) * LOG2E
    qk = (LOG2E * DA_HEAD_DIM ** 0.5 * (1.0 + 2.0 ** -6)) * jnp.max(jnp.abs(q_gain)) * jnp.max(jnp.abs(k_gain))
    bound = qk + jnp.max(b2, axis=0)
    spread = 2.0 * qk + jnp.max(jnp.max(b2, axis=0) - jnp.min(b2, axis=0))
    bounded = spread <= MAX_EXP2_SPREAD
    return jnp.where(bounded, bound, 0.0).astype(F32), bounded.astype(jnp.int32).reshape(1)


def _attn_kernel(bounded_ref, qT_ref, k_ref, vT_ref, bias_ref, subln_ref, lq1_ref, lk1_ref, lq2_ref, lk2_ref,
                 o_ref, p_scr, acc_scr):
    i = pl.program_id(1)
    t = T_ATT
    qT = qT_ref[...]
    row = lax.broadcasted_iota(jnp.int32, qT.shape, 0)
    zero = jnp.zeros_like(qT)
    q_sub = (jnp.where(row < DA_HEAD_DIM, qT, zero), jnp.where(row >= DA_HEAD_DIM, qT, zero))

    def q_block(sub, qb):
        return q_sub[sub][:, qb * t:(qb + 1) * t]

    def key_tile(j):
        return k_ref[pl.ds(pl.multiple_of(j * t, t), t), :]

    def value_tile(j):
        return vT_ref[:, pl.ds(pl.multiple_of(j * t, t), t)]

    def bias_tile(j, qb):
        return bias_ref[jnp.clip(Q_BLOCKS * i + qb - j, -1, ND_ATT) + 1]

    def finish(qb, acc1, l1, acc2, l2):
        lam = (jnp.exp(jnp.sum(lq1_ref[...] * lk1_ref[...], axis=-1, keepdims=True))
               - jnp.exp(jnp.sum(lq2_ref[...] * lk2_ref[...], axis=-1, keepdims=True)) + LAM_INIT)
        o = acc1 / l1 - lam * (acc2 / l2)
        ms = jnp.mean(o * o, axis=0, keepdims=True)
        y = o * lax.rsqrt(ms + EPS) * (subln_ref[...] * (1.0 - LAM_INIT))
        o_ref[qb * t:(qb + 1) * t, :] = y.T.astype(BF16)

    @pl.when(bounded_ref[0] == 1)
    def _():
        def probs(j, sub, qb, l8):
            s = jnp.dot(key_tile(j), q_block(sub, qb), preferred_element_type=F32)
            p = jnp.exp2(s + bias_tile(j, qb))
            l8 = l8 + jnp.sum(p.reshape(t // SUBLANES, SUBLANES, t), axis=0)
            return p.astype(BF16), l8

        def round_step(r, l8s, consume, produce):
            l8s = [list(per_sub) for per_sub in l8s]
            for u in range(ATT_UNROLL):
                for qb in range(Q_BLOCKS):
                    for sub in range(2):
                        if consume:
                            acc_scr[sub, qb] += jnp.dot(value_tile(ATT_UNROLL * (r - 1) + u), p_scr[sub, u, qb],
                                                        preferred_element_type=F32)
                        if produce:
                            p_scr[sub, u, qb], l8s[sub][qb] = probs(ATT_UNROLL * r + u, sub, qb, l8s[sub][qb])
            return tuple(tuple(per_sub) for per_sub in l8s)

        n_rounds = i + 1
        acc_scr[...] = jnp.zeros(acc_scr.shape, F32)
        zeros_l = jnp.zeros((SUBLANES, t), F32)
        l8s = ((zeros_l,) * Q_BLOCKS,) * 2
        l8s = round_step(0, l8s, consume=False, produce=True)
        l8s = lax.fori_loop(1, n_rounds, functools.partial(round_step, consume=True, produce=True), l8s)
        round_step(n_rounds, l8s, consume=True, produce=False)
        for qb in range(Q_BLOCKS):
            finish(qb, acc_scr[0, qb], jnp.sum(l8s[0][qb], axis=0, keepdims=True),
                   acc_scr[1, qb], jnp.sum(l8s[1][qb], axis=0, keepdims=True))

    @pl.when(bounded_ref[0] != 1)
    def _():
        for qb in range(Q_BLOCKS):
            def body(j, carry, qb=qb):
                kt = key_tile(j)
                vt = value_tile(j)
                b = bias_tile(j, qb)
                new = []
                for sub in range(2):
                    m, l, acc = carry[sub]
                    s = jnp.dot(kt, q_block(sub, qb), preferred_element_type=F32) + b
                    m_new = jnp.maximum(m, jnp.max(s, axis=0, keepdims=True))
                    alpha = jnp.exp2(m - m_new)
                    p = jnp.exp2(s - m_new)
                    l_new = alpha * l + jnp.sum(p, axis=0, keepdims=True)
                    acc_new = alpha * acc + jnp.dot(vt, p.astype(BF16), preferred_element_type=F32)
                    new.append((m_new, l_new, acc_new))
                return tuple(new)

            init1 = (jnp.full((1, t), NEG_BIG, F32), jnp.zeros((1, t), F32), jnp.zeros((DA_V_DIM, t), F32))
            (_, l1, acc1), (_, l2, acc2) = lax.fori_loop(0, Q_BLOCKS * i + qb + 1, body, (init1, init1))
            finish(qb, acc1, l1, acc2, l2)


def _diff_attn(bounded, lq1, lk1, lq2, lk2, outT, outN, bias, subln):
    s = outN.shape[1]
    t = T_ATT
    tq = Q_BLOCKS * t
    vec = pl.BlockSpec((1, DA_HEAD_DIM), lambda h, i: (0, 0))
    return pl.pallas_call(
        _attn_kernel,
        grid=(DA_HEADS, s // tq),
        in_specs=[
            pl.BlockSpec(memory_space=pltpu.SMEM),
            pl.BlockSpec((None, DA_V_DIM, tq), lambda h, i: (0, h, i)),
            pl.BlockSpec((None, s, DA_V_DIM), lambda h, i: (0, 0, h)),
            pl.BlockSpec((None, DA_V_DIM, s), lambda h, i: (1, h, 0)),
            pl.BlockSpec((N_BIAS_TILES, None, t, t), lambda h, i: (0, h, 0, 0)),
            pl.BlockSpec((DA_V_DIM, 1), lambda h, i: (0, 0)),
            vec, vec, vec, vec,
        ],
        out_specs=pl.BlockSpec((tq, DA_V_DIM), lambda h, i: (i, h)),
        out_shape=jax.ShapeDtypeStruct((s, ATTN_WIDTH), BF16),
        scratch_shapes=[
            pltpu.VMEM((2, ATT_UNROLL, Q_BLOCKS, t, t), BF16),
            pltpu.VMEM((2, Q_BLOCKS, DA_V_DIM, t), F32),
        ],
        compiler_params=pltpu.CompilerParams(
            dimension_semantics=("arbitrary", "arbitrary"), vmem_limit_bytes=VMEM_LIMIT),
        name="diff_attn",
    )(bounded, outT, outN, outT, bias, subln, lq1, lk1, lq2, lk2)


def _rglru_kernel(xr_ref, yr_ref, cw_ref, cb_ref, wg_ref, gab_ref, gxb_ref, lam_ref, gn_ref, o_ref,
                  xpad, hstate, a_scr, b_scr, h_scr):
    step = pl.program_id(0)
    tr = TR_RNN
    c = RNN_WIDTH

    @pl.when(step == 0)
    def _():
        xpad[0:SUBLANES, :] = jnp.zeros((SUBLANES, c), F32)
        hstate[...] = jnp.zeros((1, c), F32)

    x = xr_ref[...].astype(F32)
    xpad[SUBLANES:SUBLANES + tr, :] = x
    xc = cb_ref[...] + cw_ref[CONV_WIDTH - 1:CONV_WIDTH, :] * x
    for back in range(1, CONV_WIDTH):
        w_row = cw_ref[CONV_WIDTH - 1 - back:CONV_WIDTH - back, :]
        xc = xc + w_row * xpad[SUBLANES - back:SUBLANES - back + tr, :]
    xpad[0:SUBLANES, :] = x[tr - SUBLANES:tr, :]

    xcb = xc.astype(BF16)
    ga_parts, gx_parts = [], []
    for g in range(RNN_BLOCKS):
        gg = jnp.dot(xcb[:, g * RNN_BLOCK:(g + 1) * RNN_BLOCK], wg_ref[g], preferred_element_type=F32)
        ga_parts.append(gg[:, :RNN_BLOCK])
        gx_parts.append(gg[:, RNN_BLOCK:])
    gate_a = jax.nn.sigmoid(jnp.concatenate(ga_parts, axis=-1) + gab_ref[...])
    gate_x = jax.nn.sigmoid(jnp.concatenate(gx_parts, axis=-1) + gxb_ref[...])
    z = -lam_ref[...]
    softplus = jnp.maximum(z, 0.0) + jnp.log1p(jnp.exp(-jnp.abs(z)))
    log_a = -RG_C * gate_a * softplus
    a = jnp.exp(log_a)
    one_m_a2 = 1.0 - a * a
    root = jnp.where(one_m_a2 > 0.0, one_m_a2 * lax.rsqrt(one_m_a2), 0.0)
    u = (xc * gate_x) * root

    groups = tr // SUBLANES
    a3 = a.reshape(groups, SUBLANES, c)
    b3 = u.reshape(groups, SUBLANES, c)
    row = lax.broadcasted_iota(jnp.int32, a3.shape, 1)
    shift = 1
    while shift < SUBLANES:
        a_sh = pltpu.roll(a3, shift, axis=1)
        b_sh = pltpu.roll(b3, shift, axis=1)
        keep = row >= shift
        b3 = jnp.where(keep, a3 * b_sh + b3, b3)
        a3 = jnp.where(keep, a3 * a_sh, a3)
        shift *= 2
    a_scr[...] = a3.reshape(tr, c)
    b_scr[...] = b3.reshape(tr, c)

    def carry_body(g, h):
        r0 = pl.multiple_of(g * SUBLANES, SUBLANES)
        out = a_scr[pl.ds(r0, SUBLANES), :] * h + b_scr[pl.ds(r0, SUBLANES), :]
        h_scr[pl.ds(r0, SUBLANES), :] = out
        return out[SUBLANES - 1:SUBLANES, :]

    hstate[...] = lax.fori_loop(0, groups, carry_body, hstate[...])

    y = yr_ref[...].astype(F32)
    gelu = 0.5 * y * (1.0 + jnp.tanh(math.sqrt(2.0 / math.pi) * (y + 0.044715 * (y * y * y))))
    o_ref[...] = (_rms(h_scr[...] * gelu) * gn_ref[...]).astype(BF16)


def _rglru(outN, conv_w, conv_b, w_gates, ga_b, gx_b, rg_lambda, gn):
    s = outN.shape[1]
    c = RNN_WIDTH
    row = pl.BlockSpec((1, c), lambda t: (0, 0))
    return pl.pallas_call(
        _rglru_kernel,
        grid=(s // TR_RNN,),
        in_specs=[
            pl.BlockSpec((None, TR_RNN, c), lambda t: (1, t, 0)),
            pl.BlockSpec((None, TR_RNN, c), lambda t: (2, t, 0)),
            pl.BlockSpec((CONV_WIDTH, c), lambda t: (0, 0)),
            row,
            pl.BlockSpec((RNN_BLOCKS, RNN_BLOCK, 2 * RNN_BLOCK), lambda t: (0, 0, 0)),
            row, row, row, row,
        ],
        out_specs=pl.BlockSpec((TR_RNN, c), lambda t: (t, 0)),
        out_shape=jax.ShapeDtypeStruct((s, c), BF16),
        scratch_shapes=[
            pltpu.VMEM((TR_RNN + SUBLANES, c), F32),
            pltpu.VMEM((1, c), F32),
            pltpu.VMEM((TR_RNN, c), F32),
            pltpu.VMEM((TR_RNN, c), F32),
            pltpu.VMEM((TR_RNN, c), F32),
        ],
        compiler_params=pltpu.CompilerParams(
            dimension_semantics=("arbitrary",), vmem_limit_bytes=VMEM_LIMIT),
        name="rglru",
    )(outN, outN, conv_w, conv_b, w_gates, ga_b, gx_b, rg_lambda, gn)


def _outproj_kernel(x_ref, oa_ref, or_ref, w_ref, o_ref):
    y = jnp.dot(oa_ref[...], w_ref[0:ATTN_WIDTH, :], preferred_element_type=F32)
    y = y + jnp.dot(or_ref[...], w_ref[ATTN_WIDTH:, :], preferred_element_type=F32)
    o_ref[...] = x_ref[...] + y


def _out_proj(x, oa, orn, w):
    s = x.shape[0]
    tm = TM_PROJ
    return pl.pallas_call(
        _outproj_kernel,
        grid=(s // tm,),
        in_specs=[
            pl.BlockSpec((tm, D_MODEL), lambda i: (i, 0)),
            pl.BlockSpec((tm, ATTN_WIDTH), lambda i: (i, 0)),
            pl.BlockSpec((tm, RNN_WIDTH), lambda i: (i, 0)),
            pl.BlockSpec((ATTN_WIDTH + RNN_WIDTH, D_MODEL), lambda i: (0, 0)),
        ],
        out_specs=pl.BlockSpec((tm, D_MODEL), lambda i: (i, 0)),
        out_shape=jax.ShapeDtypeStruct((s, D_MODEL), F32),
        compiler_params=pltpu.CompilerParams(
            dimension_semantics=("arbitrary",), vmem_limit_bytes=VMEM_LIMIT),
        name="out_proj",
    )(x, oa, orn, w)


def _memkv_kernel(mem_ref, g_ref, wk_ref, wv_ref, kg_ref, kT_ref, v_ref):
    m = (_rms(mem_ref[...]) * g_ref[...]).astype(BF16)
    k = jnp.dot(m, wk_ref[...], preferred_element_type=F32)
    parts = [_rms(k[:, h * X_HEAD_DIM:(h + 1) * X_HEAD_DIM]) for h in range(X_HEADS)]
    kn = jnp.concatenate(parts, axis=-1) * kg_ref[...]
    kT_ref[...] = kn.T.astype(BF16)
    v_ref[...] = jnp.dot(m, wv_ref[...], preferred_element_type=F32).astype(BF16)


def _mem_kv(mem, g, wk, wv, kg):
    return pl.pallas_call(
        _memkv_kernel,
        out_shape=[
            jax.ShapeDtypeStruct((X_WIDTH, MEM_LEN), BF16),
            jax.ShapeDtypeStruct((MEM_LEN, X_WIDTH), BF16),
        ],
        compiler_params=pltpu.CompilerParams(vmem_limit_bytes=VMEM_LIMIT),
        name="mem_kv",
    )(mem, g, wk, wv, kg)


def _xattn_kernel(x_ref, wq_ref, kT_ref, v_ref, wo_ref, g_ref, qg_ref, o_ref):
    x = x_ref[...]
    hx = (_rms(x) * g_ref[...]).astype(BF16)
    q = jnp.dot(hx, wq_ref[...], preferred_element_type=F32)
    outs = []
    for h in range(X_HEADS):
        sl = slice(h * X_HEAD_DIM, (h + 1) * X_HEAD_DIM)
        qn = (_rms(q[:, sl]) * qg_ref[:, sl]).astype(BF16)
        s = jnp.dot(qn, kT_ref[sl, :], preferred_element_type=F32)
        p = jnp.exp2(s - jnp.max(s, axis=-1, keepdims=True))
        l = jnp.sum(p, axis=-1, keepdims=True)
        o = jnp.dot(p.astype(BF16), v_ref[:, sl], preferred_element_type=F32) / l
        outs.append(o.astype(BF16))
    o_all = jnp.concatenate(outs, axis=-1)
    o_ref[...] = x + jnp.dot(o_all, wo_ref[...], preferred_element_type=F32)


def _xattn(x, g, wq, qg, kT, v, wo):
    s = x.shape[0]
    tm = TM_PROJ
    full = lambda shape: pl.BlockSpec(shape, lambda i: tuple(0 for _ in shape))
    return pl.pallas_call(
        _xattn_kernel,
        grid=(s // tm,),
        in_specs=[
            pl.BlockSpec((tm, D_MODEL), lambda i: (i, 0)),
            full((D_MODEL, X_WIDTH)),
            full((X_WIDTH, MEM_LEN)),
            full((MEM_LEN, X_WIDTH)),
            full((X_WIDTH, D_MODEL)),
            full((1, D_MODEL)),
            full((1, X_WIDTH)),
        ],
        out_specs=pl.BlockSpec((tm, D_MODEL), lambda i: (i, 0)),
        out_shape=jax.ShapeDtypeStruct((s, D_MODEL), F32),
        compiler_params=pltpu.CompilerParams(
            dimension_semantics=("arbitrary",), vmem_limit_bytes=VMEM_LIMIT),
        name="xattn",
    )(x, wq, kT, v, wo, g, qg)


def _ffn_kernel(x_ref, wg_ref, wu_ref, wd_ref, g_ref, o_ref, h_scr):
    f = pl.program_id(1)

    @pl.when(f == 0)
    def _():
        x = x_ref[...]
        h_scr[...] = (_rms(x) * g_ref[...]).astype(BF16)
        o_ref[...] = x

    h = h_scr[...]
    gate = jnp.dot(h, wg_ref[...], preferred_element_type=F32)
    up = jnp.dot(h, wu_ref[...], preferred_element_type=F32)
    act = (gate * jax.nn.sigmoid(gate) * up).astype(BF16)
    o_ref[...] += jnp.dot(act, wd_ref[...], preferred_element_type=F32)


def _ffn(x, g, w_gate_up, w_down):
    s = x.shape[0]
    tm = TM_FFN
    nf = D_FF // TF_FFN
    return pl.pallas_call(
        _ffn_kernel,
        grid=(s // tm, nf),
        in_specs=[
            pl.BlockSpec((tm, D_MODEL), lambda i, f: (i, 0)),
            pl.BlockSpec((D_MODEL, TF_FFN), lambda i, f: (0, f)),
            pl.BlockSpec((D_MODEL, TF_FFN), lambda i, f: (0, nf + f)),
            pl.BlockSpec((TF_FFN, D_MODEL), lambda i, f: (f, 0)),
            pl.BlockSpec((1, D_MODEL), lambda i, f: (0, 0)),
        ],
        out_specs=pl.BlockSpec((tm, D_MODEL), lambda i, f: (i, 0)),
        out_shape=jax.ShapeDtypeStruct((s, D_MODEL), F32),
        scratch_shapes=[pltpu.VMEM((tm, D_MODEL), BF16)],
        compiler_params=pltpu.CompilerParams(
            dimension_semantics=("arbitrary", "arbitrary"), vmem_limit_bytes=VMEM_LIMIT),
        name="ffn",
    )(x, w_gate_up, w_gate_up, w_down, g)


def kernel(x, mem, rel_bias, attn_norm, w_in, da_q_norm, da_k_norm, da_lambda_q1, da_lambda_k1, da_lambda_q2, da_lambda_k2, da_subln, conv_w, conv_b, gate_a_w, gate_a_b, gate_x_w, gate_x_b, rg_lambda, rnn_out_norm, w_out, xattn_norm, mem_norm, xq_w, xk_w, xv_w, xq_norm, xk_norm, xo_w, ffn_norm, w_gate_up, w_down):
    assert x.shape == (1, SEQ, D_MODEL) and mem.shape == (1, MEM_LEN, D_MODEL)
    assert w_in.shape == (1, D_MODEL, 3 * ATTN_WIDTH + 2 * RNN_WIDTH)
    assert w_gate_up.shape == (1, D_MODEL, 2 * D_FF) and w_down.shape == (1, D_FF, D_MODEL)
    assert SEQ % (T_ATT * Q_BLOCKS) == 0 and SEQ % TM_PROJ == 0 and SEQ % TM_IN == 0 and SEQ % TM_FFN == 0
    assert SEQ % TR_RNN == 0 and D_FF % TF_FFN == 0
    x2d = x[0]
    row = lambda v: v.reshape(1, -1).astype(F32)

    a = ATTN_WIDTH
    wi = w_in[0]
    w_in_b = jnp.concatenate([wi[:, 0:a], wi[:, 2 * a:3 * a], wi[:, a:2 * a], wi[:, 3 * a:]], axis=1).astype(BF16)
    grp = jnp.arange(MXU_DIM) // DA_HEAD_DIM
    gsum = jnp.where(grp[:, None] == grp[None, :], 1.0 / DA_HEAD_DIM, 0.0).astype(BF16)
    reps = ATTN_WIDTH // DA_HEAD_DIM
    qg = row(jnp.tile(da_q_norm[0], reps)) * (DA_HEAD_DIM ** -0.5 * LOG2E)
    kg = row(jnp.tile(da_k_norm[0], reps))

    outT, outN = _in_proj(x2d, row(attn_norm[0]), w_in_b, gsum, qg, kg)

    shift, bounded = _logit_bound(rel_bias, da_q_norm[0], da_k_norm[0])
    bias = _bias_tiles(rel_bias.astype(F32), shift)
    o_attn = _diff_attn(bounded, row(da_lambda_q1[0]), row(da_lambda_k1[0]), row(da_lambda_q2[0]), row(da_lambda_k2[0]),
                        outT, outN, bias, da_subln[0].reshape(DA_V_DIM, 1).astype(F32))

    w_gates = jnp.concatenate([gate_a_w[0], gate_x_w[0]], axis=-1).astype(BF16)
    o_rnn = _rglru(outN, conv_w[0].astype(F32), row(conv_b[0]), w_gates, row(gate_a_b[0]), row(gate_x_b[0]),
                   row(rg_lambda[0]), row(rnn_out_norm[0]))

    x1 = _out_proj(x2d, o_attn, o_rnn, w_out[0].astype(BF16))

    kT, vm = _mem_kv(mem[0], row(mem_norm[0]), xk_w[0].astype(BF16), xv_w[0].astype(BF16),
                     row(jnp.tile(xk_norm[0], X_HEADS)))
    xqg = row(jnp.tile(xq_norm[0], X_HEADS)) * (X_HEAD_DIM ** -0.5 * LOG2E)
    x2 = _xattn(x1, row(xattn_norm[0]), xq_w[0].astype(BF16), xqg, kT, vm, xo_w[0].astype(BF16))

    x3 = _ffn(x2, row(ffn_norm[0]), w_gate_up[0].astype(BF16), w_down[0].astype(BF16))
    return x3[None]
```

```python
import functools
import math

import jax
import jax.numpy as jnp
from jax import lax
from jax.experimental import pallas as pl
from jax.experimental.pallas import tpu as pltpu

F32 = jnp.float32
BF16 = jnp.bfloat16

D_MODEL = 2048
SEQ = 16384
CHUNK = 64
CHUNK_SHIFT = CHUNK.bit_length() - 1
assert 1 << CHUNK_SHIFT == CHUNK
MEM_LEN = 256
EPS = 1e-6
ATTN_WIDTH = 1024
DA_HEAD_DIM = 64
DA_HEADS = 8
DA_V_DIM = 2 * DA_HEAD_DIM
RNN_WIDTH = 1024
RNN_BLOCKS = 8
RNN_BLOCK = RNN_WIDTH // RNN_BLOCKS
CONV_WIDTH = 4
RG_C = 8.0
X_HEADS = 4
X_HEAD_DIM = 128
X_WIDTH = X_HEADS * X_HEAD_DIM
D_FF = 5632
NUM_BUCKETS = 32
MAX_DISTANCE = 1024
LAM_INIT = 0.8 - 0.6 * math.exp(-0.3 * 0)
LOG2E = 1.4426950408889634
NEG_BIG = -1e30
MAX_EXP2_SPREAD = 120.0

LANES = 128
SUBLANES = 8
MXU_DIM = 256
VMEM_LIMIT = 56 * 1024 * 1024

TM_PROJ = 512
TM_IN = 1024
TM_FFN = 1024
SEG = 1024
T_ATT = 512
TR_RNN = 256
TF_FFN = 512


def _far_bucket_distance():
    half = NUM_BUCKETS // 2
    max_exact = half // 2
    n = max_exact
    while max_exact + int(math.log(n / max_exact) / math.log(MAX_DISTANCE / max_exact) * (half - max_exact)) < half - 1:
        n += 1
    return n


FAR_DIST = _far_bucket_distance()
ND_ATT = next(d for d in range(1, 64) if (d - 1) * T_ATT + 1 >= FAR_DIST + 64)
FAR_BUCKET = NUM_BUCKETS // 2 - 1
N_BIAS_TILES = ND_ATT + 2
ATT_UNROLL = 4
Q_BLOCKS = 4
assert Q_BLOCKS == ATT_UNROLL


def _rms(x, eps=EPS):
    return x * lax.rsqrt(jnp.mean(x * x, axis=-1, keepdims=True) + eps)


def _inproj_kernel(x_ref, w_ref, gsum_ref, g_ref, qg_ref, kg_ref, outT_ref, outN_ref, h_scr):
    j = pl.program_id(1)

    @pl.when(j == 0)
    def _():
        h_scr[...] = (_rms(x_ref[...]) * g_ref[...]).astype(BF16)

    acc = jnp.dot(h_scr[...], w_ref[...], preferred_element_type=F32)

    def group_norm(a, gain_row):
        sq = (a * a).astype(BF16)
        parts = [jnp.dot(sq[:, c * MXU_DIM:(c + 1) * MXU_DIM], gsum_ref[...], preferred_element_type=F32)
                 for c in range(SEG // MXU_DIM)]
        ms = jnp.concatenate(parts, axis=-1)
        return a * lax.rsqrt(ms + EPS) * gain_row

    @pl.when(j == 0)
    def _():
        outT_ref[...] = group_norm(acc, qg_ref[...]).T.astype(BF16)

    @pl.when(j == 1)
    def _():
        outT_ref[...] = acc.T.astype(BF16)

    @pl.when(j == 2)
    def _():
        outN_ref[...] = group_norm(acc, kg_ref[...]).astype(BF16)

    @pl.when(j >= 3)
    def _():
        outN_ref[...] = acc.astype(BF16)


def _in_proj(x, g, w, gsum, qg, kg):
    s = x.shape[0]
    nseg = w.shape[1] // SEG
    return pl.pallas_call(
        _inproj_kernel,
        grid=(s // TM_IN, nseg),
        in_specs=[
            pl.BlockSpec((TM_IN, D_MODEL), lambda i, j: (i, 0)),
            pl.BlockSpec((D_MODEL, SEG), lambda i, j: (0, j)),
            pl.BlockSpec((MXU_DIM, MXU_DIM), lambda i, j: (0, 0)),
            pl.BlockSpec((1, D_MODEL), lambda i, j: (0, 0)),
            pl.BlockSpec((1, SEG), lambda i, j: (0, 0)),
            pl.BlockSpec((1, SEG), lambda i, j: (0, 0)),
        ],
        out_specs=[
            pl.BlockSpec((None, SEG, TM_IN), lambda i, j: (jnp.minimum(j, 1), 0, i)),
            pl.BlockSpec((None, TM_IN, SEG), lambda i, j: (jnp.maximum(j - 2, 0), i, 0)),
        ],
        out_shape=[
            jax.ShapeDtypeStruct((2, SEG, s), BF16),
            jax.ShapeDtypeStruct((3, s, SEG), BF16),
        ],
        scratch_shapes=[pltpu.VMEM((TM_IN, D_MODEL), BF16)],
        compiler_params=pltpu.CompilerParams(
            dimension_semantics=("arbitrary", "arbitrary"), vmem_limit_bytes=VMEM_LIMIT),
        name="in_proj",
    )(x, w, gsum, g, qg, kg)


def _bias_kernel(rb_ref, shift_ref, out_ref, bucket_scr):
    d = pl.program_id(0) - 1
    t = T_ATT
    half = NUM_BUCKETS // 2

    def fill(n_buckets):
        def head(h, carry):
            bucket = bucket_scr[...]
            val = jnp.zeros((t, t), F32)
            for b in range(n_buckets):
                val = jnp.where(bucket == b, rb_ref[b, h], val)
            val = (val - rb_ref[FAR_BUCKET, h]) * LOG2E - shift_ref[h]
            out_ref[h] = jnp.where(bucket >= 0, val, NEG_BIG)
            return carry
        lax.fori_loop(0, DA_HEADS, head, 0)

    @pl.when(jnp.logical_and(d >= 0, d < ND_ATT))
    def _():
        kk = lax.broadcasted_iota(jnp.int32, (t, t), 0)
        qq = lax.broadcasted_iota(jnp.int32, (t, t), 1)
        rel = kk - qq - d * t
        max_exact = half // 2
        n = jnp.abs(rel)
        large = max_exact + (jnp.log(jnp.maximum(n, 1).astype(F32) / max_exact)
                             / math.log(MAX_DISTANCE / max_exact) * (half - max_exact)).astype(jnp.int32)
        large = jnp.minimum(large, half - 1)
        bucket = jnp.where(rel > 0, half, 0) + jnp.where(n < max_exact, n, large)
        allowed = (kk >> CHUNK_SHIFT) <= ((qq + d * t) >> CHUNK_SHIFT)
        bucket_scr[...] = jnp.where(allowed, bucket, -1)

    @pl.when(d == 0)
    def _():
        fill(NUM_BUCKETS)

    @pl.when(jnp.logical_and(d >= 1, d < ND_ATT))
    def _():
        fill(half)

    @pl.when(jnp.logical_or(d < 0, d >= ND_ATT))
    def _():
        def head(h, carry):
            out_ref[h] = jnp.full((t, t), jnp.where(d < 0, NEG_BIG, -shift_ref[h]), F32)
            return carry
        lax.fori_loop(0, DA_HEADS, head, 0)


def _bias_tiles(rel_bias, shift):
    return pl.pallas_call(
        _bias_kernel,
        grid=(N_BIAS_TILES,),
        in_specs=[pl.BlockSpec(memory_space=pltpu.SMEM), pl.BlockSpec(memory_space=pltpu.SMEM)],
        out_specs=pl.BlockSpec((None, DA_HEADS, T_ATT, T_ATT), lambda d: (d, 0, 0, 0)),
        out_shape=jax.ShapeDtypeStruct((N_BIAS_TILES, DA_HEADS, T_ATT, T_ATT), F32),
        scratch_shapes=[pltpu.VMEM((T_ATT, T_ATT), jnp.int32)],
        compiler_params=pltpu.CompilerParams(
            dimension_semantics=("arbitrary",), vmem_limit_bytes=VMEM_LIMIT),
        name="bias_tiles",
    )(rel_bias, shift)


def _logit_bound(rel_bias, q_gain, k_gain):
    rb = rel_bias.astype(F32)
    b2 = (rb - rb[FAR_BUCKET]) * LOG2E
    qk = (LOG2E * DA_HEAD_DIM ** 0.5 * (1.0 + 2.0 ** -6)) * jnp.max(jnp.abs(q_gain)) * jnp.max(jnp.abs(k_gain))
    bound = qk + jnp.max(b2, axis=0)
    spread = 2.0 * qk + jnp.max(jnp.max(b2, axis=0) - jnp.min(b2, axis=0))
    bounded = spread <= MAX_EXP2_SPREAD
    return jnp.where(bounded, bound, 0.0).astype(F32), bounded.astype(jnp.int32).reshape(1)


def _attn_kernel(bounded_ref, qT_ref, k_ref, vT_ref, bias_ref, subln_ref, lq1_ref, lk1_ref, lq2_ref, lk2_ref,
                 o_ref, p_scr, acc_scr, l_scr):
    i = pl.program_id(1)
    t = T_ATT

    def block_rows(qb):
        return pl.ds(qb * t if isinstance(qb, int) else pl.multiple_of(qb * t, t), t)

    def q_block(sub, qb):
        q = qT_ref[:, block_rows(qb)]
        row = lax.broadcasted_iota(jnp.int32, q.shape, 0)
        keep = row < DA_HEAD_DIM if sub == 0 else row >= DA_HEAD_DIM
        return jnp.where(keep, q, jnp.zeros_like(q))

    def key_tile(j):
        return k_ref[pl.ds(pl.multiple_of(j * t, t), t), :]

    def value_tile(j):
        return vT_ref[:, pl.ds(pl.multiple_of(j * t, t), t)]

    def bias_tile(j, qb):
        return bias_ref[jnp.clip(Q_BLOCKS * i + qb - j, -1, ND_ATT) + 1]

    def finish(qb, acc1, l1, acc2, l2):
        lam = (jnp.exp(jnp.sum(lq1_ref[...] * lk1_ref[...], axis=-1, keepdims=True))
               - jnp.exp(jnp.sum(lq2_ref[...] * lk2_ref[...], axis=-1, keepdims=True)) + LAM_INIT)
        o = acc1 / l1 - lam * (acc2 / l2)
        ms = jnp.mean(o * o, axis=0, keepdims=True)
        y = o * lax.rsqrt(ms + EPS) * (subln_ref[...] * (1.0 - LAM_INIT))
        o_ref[block_rows(qb), :] = y.T.astype(BF16)

    @pl.when(bounded_ref[0] == 1)
    def _():
        def probs(j, sub, qb, bias_index):
            s = jnp.dot(key_tile(j), q_block(sub, qb), preferred_element_type=F32)
            p = jnp.exp2(s + bias_ref[bias_index])
            l_scr[sub, qb] += jnp.sum(p.reshape(t // SUBLANES, SUBLANES, t), axis=0)
            return p.astype(BF16)

        def weighted(j, sub, qb, p):
            acc_scr[sub, qb] += jnp.dot(value_tile(j), p, preferred_element_type=F32)

        def step(r, consume, produce):
            for u in range(ATT_UNROLL):
                for qb in range(Q_BLOCKS):
                    for sub in range(2):
                        if consume:
                            weighted(ATT_UNROLL * (r - 1) + u, sub, qb, p_scr[sub, u, qb])
                        if produce:
                            j = ATT_UNROLL * r + u
                            p_scr[sub, u, qb] = probs(j, sub, qb, jnp.clip(Q_BLOCKS * i + qb - j, -1, ND_ATT) + 1)

        def full_step(r, carry):
            step(r, True, True)
            return carry

        acc_scr[...] = jnp.zeros(acc_scr.shape, F32)
        l_scr[...] = jnp.zeros(l_scr.shape, F32)

        @pl.when(i > 0)
        def _():
            step(0, False, True)
            lax.fori_loop(1, i, full_step, 0)
            step(i, True, False)

        for u in range(ATT_UNROLL):
            for qb in range(u, Q_BLOCKS):
                for sub in range(2):
                    j = ATT_UNROLL * i + u
                    weighted(j, sub, qb, probs(j, sub, qb, min(qb - u, ND_ATT) + 1))

        for qb in range(Q_BLOCKS):
            finish(qb, acc_scr[0, qb], jnp.sum(l_scr[0, qb], axis=0, keepdims=True),
                   acc_scr[1, qb], jnp.sum(l_scr[1, qb], axis=0, keepdims=True))

    @pl.when(bounded_ref[0] != 1)
    def _():
        def query_block(qb, carry):
            q_sub = (q_block(0, qb), q_block(1, qb))

            def body(j, state):
                kt = key_tile(j)
                vt = value_tile(j)
                b = bias_tile(j, qb)
                new = []
                for sub in range(2):
                    m, l, acc = state[sub]
                    s = jnp.dot(kt, q_sub[sub], preferred_element_type=F32) + b
                    m_new = jnp.maximum(m, jnp.max(s, axis=0, keepdims=True))
                    alpha = jnp.exp2(m - m_new)
                    p = jnp.exp2(s - m_new)
                    l_new = alpha * l + jnp.sum(p, axis=0, keepdims=True)
                    acc_new = alpha * acc + jnp.dot(vt, p.astype(BF16), preferred_element_type=F32)
                    new.append((m_new, l_new, acc_new))
                return tuple(new)

            init1 = (jnp.full((1, t), NEG_BIG, F32), jnp.zeros((1, t), F32), jnp.zeros((DA_V_DIM, t), F32))
            (_, l1, acc1), (_, l2, acc2) = lax.fori_loop(0, Q_BLOCKS * i + qb + 1, body, (init1, init1))
            finish(qb, acc1, l1, acc2, l2)
            return carry

        lax.fori_loop(0, Q_BLOCKS, query_block, 0)


def _diff_attn(bounded, lq1, lk1, lq2, lk2, outT, outN, bias, subln):
    s = outN.shape[1]
    t = T_ATT
    tq = Q_BLOCKS * t
    vec = pl.BlockSpec((1, DA_HEAD_DIM), lambda h, i: (0, 0))
    return pl.pallas_call(
        _attn_kernel,
        grid=(DA_HEADS, s // tq),
        in_specs=[
            pl.BlockSpec(memory_space=pltpu.SMEM),
            pl.BlockSpec((None, DA_V_DIM, tq), lambda h, i: (0, h, i)),
            pl.BlockSpec((None, s, DA_V_DIM), lambda h, i: (0, 0, h)),
            pl.BlockSpec((None, DA_V_DIM, s), lambda h, i: (1, h, 0)),
            pl.BlockSpec((N_BIAS_TILES, None, t, t), lambda h, i: (0, h, 0, 0)),
            pl.BlockSpec((DA_V_DIM, 1), lambda h, i: (0, 0)),
            vec, vec, vec, vec,
        ],
        out_specs=pl.BlockSpec((tq, DA_V_DIM), lambda h, i: (i, h)),
        out_shape=jax.ShapeDtypeStruct((s, ATTN_WIDTH), BF16),
        scratch_shapes=[
            pltpu.VMEM((2, ATT_UNROLL, Q_BLOCKS, t, t), BF16),
            pltpu.VMEM((2, Q_BLOCKS, DA_V_DIM, t), F32),
            pltpu.VMEM((2, Q_BLOCKS, SUBLANES, t), F32),
        ],
        compiler_params=pltpu.CompilerParams(
            dimension_semantics=("arbitrary", "arbitrary"), vmem_limit_bytes=VMEM_LIMIT),
        name="diff_attn",
    )(bounded, outT, outN, outT, bias, subln, lq1, lk1, lq2, lk2)


def _rglru_kernel(xr_ref, yr_ref, cw_ref, cb_ref, wg_ref, gab_ref, gxb_ref, lam_ref, gn_ref, o_ref,
                  xpad, hstate, a_scr, b_scr, h_scr):
    step = pl.program_id(0)
    tr = TR_RNN
    c = RNN_WIDTH

    @pl.when(step == 0)
    def _():
        xpad[0:SUBLANES, :] = jnp.zeros((SUBLANES, c), F32)
        hstate[...] = jnp.zeros((1, c), F32)

    x = xr_ref[...].astype(F32)
    xpad[SUBLANES:SUBLANES + tr, :] = x
    xc = cb_ref[...] + cw_ref[CONV_WIDTH - 1:CONV_WIDTH, :] * x
    for back in range(1, CONV_WIDTH):
        w_row = cw_ref[CONV_WIDTH - 1 - back:CONV_WIDTH - back, :]
        xc = xc + w_row * xpad[SUBLANES - back:SUBLANES - back + tr, :]
    xpad[0:SUBLANES, :] = x[tr - SUBLANES:tr, :]

    xcb = xc.astype(BF16)
    ga_parts, gx_parts = [], []
    for g in range(RNN_BLOCKS):
        gg = jnp.dot(xcb[:, g * RNN_BLOCK:(g + 1) * RNN_BLOCK], wg_ref[g], preferred_element_type=F32)
        ga_parts.append(gg[:, :RNN_BLOCK])
        gx_parts.append(gg[:, RNN_BLOCK:])
    gate_a = jax.nn.sigmoid(jnp.concatenate(ga_parts, axis=-1) + gab_ref[...])
    gate_x = jax.nn.sigmoid(jnp.concatenate(gx_parts, axis=-1) + gxb_ref[...])
    z = -lam_ref[...]
    softplus = jnp.maximum(z, 0.0) + jnp.log1p(jnp.exp(-jnp.abs(z)))
    log_a = -RG_C * gate_a * softplus
    a = jnp.exp(log_a)
    one_m_a2 = 1.0 - a * a
    root = jnp.where(one_m_a2 > 0.0, one_m_a2 * lax.rsqrt(one_m_a2), 0.0)
    u = (xc * gate_x) * root

    groups = tr // SUBLANES
    a3 = a.reshape(groups, SUBLANES, c)
    b3 = u.reshape(groups, SUBLANES, c)
    row = lax.broadcasted_iota(jnp.int32, a3.shape, 1)
    shift = 1
    while shift < SUBLANES:
        a_sh = pltpu.roll(a3, shift, axis=1)
        b_sh = pltpu.roll(b3, shift, axis=1)
        keep = row >= shift
        b3 = jnp.where(keep, a3 * b_sh + b3, b3)
        a3 = jnp.where(keep, a3 * a_sh, a3)
        shift *= 2
    a_scr[...] = a3.reshape(tr, c)
    b_scr[...] = b3.reshape(tr, c)

    def carry_body(g, h):
        r0 = pl.multiple_of(g * SUBLANES, SUBLANES)
        out = a_scr[pl.ds(r0, SUBLANES), :] * h + b_scr[pl.ds(r0, SUBLANES), :]
        h_scr[pl.ds(r0, SUBLANES), :] = out
        return out[SUBLANES - 1:SUBLANES, :]

    hstate[...] = lax.fori_loop(0, groups, carry_body, hstate[...])

    y = yr_ref[...].astype(F32)
    gelu = 0.5 * y * (1.0 + jnp.tanh(math.sqrt(2.0 / math.pi) * (y + 0.044715 * (y * y * y))))
    o_ref[...] = (_rms(h_scr[...] * gelu) * gn_ref[...]).astype(BF16)


def _rglru(outN, conv_w, conv_b, w_gates, ga_b, gx_b, rg_lambda, gn):
    s = outN.shape[1]
    c = RNN_WIDTH
    row = pl.BlockSpec((1, c), lambda t: (0, 0))
    return pl.pallas_call(
        _rglru_kernel,
        grid=(s // TR_RNN,),
        in_specs=[
            pl.BlockSpec((None, TR_RNN, c), lambda t: (1, t, 0)),
            pl.BlockSpec((None, TR_RNN, c), lambda t: (2, t, 0)),
            pl.BlockSpec((CONV_WIDTH, c), lambda t: (0, 0)),
            row,
            pl.BlockSpec((RNN_BLOCKS, RNN_BLOCK, 2 * RNN_BLOCK), lambda t: (0, 0, 0)),
            row, row, row, row,
        ],
        out_specs=pl.BlockSpec((TR_RNN, c), lambda t: (t, 0)),
        out_shape=jax.ShapeDtypeStruct((s, c), BF16),
        scratch_shapes=[
            pltpu.VMEM((TR_RNN + SUBLANES, c), F32),
            pltpu.VMEM((1, c), F32),
            pltpu.VMEM((TR_RNN, c), F32),
            pltpu.VMEM((TR_RNN, c), F32),
            pltpu.VMEM((TR_RNN, c), F32),
        ],
        compiler_params=pltpu.CompilerParams(
            dimension_semantics=("arbitrary",), vmem_limit_bytes=VMEM_LIMIT),
        name="rglru",
    )(outN, outN, conv_w, conv_b, w_gates, ga_b, gx_b, rg_lambda, gn)


def _outproj_kernel(x_ref, oa_ref, or_ref, w_ref, o_ref):
    y = jnp.dot(oa_ref[...], w_ref[0:ATTN_WIDTH, :], preferred_element_type=F32)
    y = y + jnp.dot(or_ref[...], w_ref[ATTN_WIDTH:, :], preferred_element_type=F32)
    o_ref[...] = x_ref[...] + y


def _out_proj(x, oa, orn, w):
    s = x.shape[0]
    tm = TM_PROJ
    return pl.pallas_call(
        _outproj_kernel,
        grid=(s // tm,),
        in_specs=[
            pl.BlockSpec((tm, D_MODEL), lambda i: (i, 0)),
            pl.BlockSpec((tm, ATTN_WIDTH), lambda i: (i, 0)),
            pl.BlockSpec((tm, RNN_WIDTH), lambda i: (i, 0)),
            pl.BlockSpec((ATTN_WIDTH + RNN_WIDTH, D_MODEL), lambda i: (0, 0)),
        ],
        out_specs=pl.BlockSpec((tm, D_MODEL), lambda i: (i, 0)),
        out_shape=jax.ShapeDtypeStruct((s, D_MODEL), F32),
        compiler_params=pltpu.CompilerParams(
            dimension_semantics=("arbitrary",), vmem_limit_bytes=VMEM_LIMIT),
        name="out_proj",
    )(x, oa, orn, w)


def _memkv_kernel(mem_ref, g_ref, wk_ref, wv_ref, kg_ref, kT_ref, v_ref):
    m = (_rms(mem_ref[...]) * g_ref[...]).astype(BF16)
    k = jnp.dot(m, wk_ref[...], preferred_element_type=F32)
    parts = [_rms(k[:, h * X_HEAD_DIM:(h + 1) * X_HEAD_DIM]) for h in range(X_HEADS)]
    kn = jnp.concatenate(parts, axis=-1) * kg_ref[...]
    kT_ref[...] = kn.T.astype(BF16)
    v_ref[...] = jnp.dot(m, wv_ref[...], preferred_element_type=F32).astype(BF16)


def _mem_kv(mem, g, wk, wv, kg):
    return pl.pallas_call(
        _memkv_kernel,
        out_shape=[
            jax.ShapeDtypeStruct((X_WIDTH, MEM_LEN), BF16),
            jax.ShapeDtypeStruct((MEM_LEN, X_WIDTH), BF16),
        ],
        compiler_params=pltpu.CompilerParams(vmem_limit_bytes=VMEM_LIMIT),
        name="mem_kv",
    )(mem, g, wk, wv, kg)


def _xattn_kernel(x_ref, wq_ref, kT_ref, v_ref, wo_ref, g_ref, qg_ref, o_ref):
    x = x_ref[...]
    hx = (_rms(x) * g_ref[...]).astype(BF16)
    q = jnp.dot(hx, wq_ref[...], preferred_element_type=F32)
    outs = []
    for h in range(X_HEADS):
        sl = slice(h * X_HEAD_DIM, (h + 1) * X_HEAD_DIM)
        qn = (_rms(q[:, sl]) * qg_ref[:, sl]).astype(BF16)
        s = jnp.dot(qn, kT_ref[sl, :], preferred_element_type=F32)
        p = jnp.exp2(s - jnp.max(s, axis=-1, keepdims=True))
        l = jnp.sum(p, axis=-1, keepdims=True)
        o = jnp.dot(p.astype(BF16), v_ref[:, sl], preferred_element_type=F32) / l
        outs.append(o.astype(BF16))
    o_all = jnp.concatenate(outs, axis=-1)
    o_ref[...] = x + jnp.dot(o_all, wo_ref[...], preferred_element_type=F32)


def _xattn(x, g, wq, qg, kT, v, wo):
    s = x.shape[0]
    tm = TM_PROJ
    full = lambda shape: pl.BlockSpec(shape, lambda i: tuple(0 for _ in shape))
    return pl.pallas_call(
        _xattn_kernel,
        grid=(s // tm,),
        in_specs=[
            pl.BlockSpec((tm, D_MODEL), lambda i: (i, 0)),
            full((D_MODEL, X_WIDTH)),
            full((X_WIDTH, MEM_LEN)),
            full((MEM_LEN, X_WIDTH)),
            full((X_WIDTH, D_MODEL)),
            full((1, D_MODEL)),
            full((1, X_WIDTH)),
        ],
        out_specs=pl.BlockSpec((tm, D_MODEL), lambda i: (i, 0)),
        out_shape=jax.ShapeDtypeStruct((s, D_MODEL), F32),
        compiler_params=pltpu.CompilerParams(
            dimension_semantics=("arbitrary",), vmem_limit_bytes=VMEM_LIMIT),
        name="xattn",
    )(x, wq, kT, v, wo, g, qg)


def _ffn_kernel(x_ref, wg_ref, wu_ref, wd_ref, g_ref, o_ref, h_scr):
    f = pl.program_id(1)

    @pl.when(f == 0)
    def _():
        x = x_ref[...]
        h_scr[...] = (_rms(x) * g_ref[...]).astype(BF16)
        o_ref[...] = x

    h = h_scr[...]
    gate = jnp.dot(h, wg_ref[...], preferred_element_type=F32)
    up = jnp.dot(h, wu_ref[...], preferred_element_type=F32)
    act = (gate * jax.nn.sigmoid(gate) * up).astype(BF16)
    o_ref[...] += jnp.dot(act, wd_ref[...], preferred_element_type=F32)


def _ffn(x, g, w_gate_up, w_down):
    s = x.shape[0]
    tm = TM_FFN
    nf = D_FF // TF_FFN
    return pl.pallas_call(
        _ffn_kernel,
        grid=(s // tm, nf),
        in_specs=[
            pl.BlockSpec((tm, D_MODEL), lambda i, f: (i, 0)),
            pl.BlockSpec((D_MODEL, TF_FFN), lambda i, f: (0, f)),
            pl.BlockSpec((D_MODEL, TF_FFN), lambda i, f: (0, nf + f)),
            pl.BlockSpec((TF_FFN, D_MODEL), lambda i, f: (f, 0)),
            pl.BlockSpec((1, D_MODEL), lambda i, f: (0, 0)),
        ],
        out_specs=pl.BlockSpec((tm, D_MODEL), lambda i, f: (i, 0)),
        out_shape=jax.ShapeDtypeStruct((s, D_MODEL), F32),
        scratch_shapes=[pltpu.VMEM((tm, D_MODEL), BF16)],
        compiler_params=pltpu.CompilerParams(
            dimension_semantics=("arbitrary", "arbitrary"), vmem_limit_bytes=VMEM_LIMIT),
        name="ffn",
    )(x, w_gate_up, w_gate_up, w_down, g)


def kernel(x, mem, rel_bias, attn_norm, w_in, da_q_norm, da_k_norm, da_lambda_q1, da_lambda_k1, da_lambda_q2, da_lambda_k2, da_subln, conv_w, conv_b, gate_a_w, gate_a_b, gate_x_w, gate_x_b, rg_lambda, rnn_out_norm, w_out, xattn_norm, mem_norm, xq_w, xk_w, xv_w, xq_norm, xk_norm, xo_w, ffn_norm, w_gate_up, w_down):
    assert x.shape == (1, SEQ, D_MODEL) and mem.shape == (1, MEM_LEN, D_MODEL)
    assert w_in.shape == (1, D_MODEL, 3 * ATTN_WIDTH + 2 * RNN_WIDTH)
    assert w_gate_up.shape == (1, D_MODEL, 2 * D_FF) and w_down.shape == (1, D_FF, D_MODEL)
    assert SEQ % (T_ATT * Q_BLOCKS) == 0 and SEQ % TM_PROJ == 0 and SEQ % TM_IN == 0 and SEQ % TM_FFN == 0
    assert SEQ % TR_RNN == 0 and D_FF % TF_FFN == 0
    x2d = x[0]
    row = lambda v: v.reshape(1, -1).astype(F32)

    a = ATTN_WIDTH
    wi = w_in[0]
    w_in_b = jnp.concatenate([wi[:, 0:a], wi[:, 2 * a:3 * a], wi[:, a:2 * a], wi[:, 3 * a:]], axis=1).astype(BF16)
    grp = jnp.arange(MXU_DIM) // DA_HEAD_DIM
    gsum = jnp.where(grp[:, None] == grp[None, :], 1.0 / DA_HEAD_DIM, 0.0).astype(BF16)
    reps = ATTN_WIDTH // DA_HEAD_DIM
    qg = row(jnp.tile(da_q_norm[0], reps)) * (DA_HEAD_DIM ** -0.5 * LOG2E)
    kg = row(jnp.tile(da_k_norm[0], reps))

    outT, outN = _in_proj(x2d, row(attn_norm[0]), w_in_b, gsum, qg, kg)

    shift, bounded = _logit_bound(rel_bias, da_q_norm[0], da_k_norm[0])
    bias = _bias_tiles(rel_bias.astype(F32), shift)
    o_attn = _diff_attn(bounded, row(da_lambda_q1[0]), row(da_lambda_k1[0]), row(da_lambda_q2[0]), row(da_lambda_k2[0]),
                        outT, outN, bias, da_subln[0].reshape(DA_V_DIM, 1).astype(F32))

    w_gates = jnp.concatenate([gate_a_w[0], gate_x_w[0]], axis=-1).astype(BF16)
    o_rnn = _rglru(outN, conv_w[0].astype(F32), row(conv_b[0]), w_gates, row(gate_a_b[0]), row(gate_x_b[0]),
                   row(rg_lambda[0]), row(rnn_out_norm[0]))

    x1 = _out_proj(x2d, o_attn, o_rnn, w_out[0].astype(BF16))

    kT, vm = _mem_kv(mem[0], row(mem_norm[0]), xk_w[0].astype(BF16), xv_w[0].astype(BF16),
                     row(jnp.tile(xk_norm[0], X_HEADS)))
    xqg = row(jnp.tile(xq_norm[0], X_HEADS)) * (X_HEAD_DIM ** -0.5 * LOG2E)
    x2 = _xattn(x1, row(xattn_norm[0]), xq_w[0].astype(BF16), xqg, kT, vm, xo_w[0].astype(BF16))

    x3 = _ffn(x2, row(ffn_norm[0]), w_gate_up[0].astype(BF16), w_down[0].astype(BF16))
    return x3[None]
```

```python
import functools
import math

import jax
import jax.numpy as jnp
from jax import lax
from jax.experimental import pallas as pl
from jax.experimental.pallas import tpu as pltpu

F32 = jnp.float32
BF16 = jnp.bfloat16

D_MODEL = 2048
SEQ = 16384
CHUNK = 64
CHUNK_SHIFT = CHUNK.bit_length() - 1
assert 1 << CHUNK_SHIFT == CHUNK
MEM_LEN = 256
EPS = 1e-6
ATTN_WIDTH = 1024
DA_HEAD_DIM = 64
DA_HEADS = 8
DA_V_DIM = 2 * DA_HEAD_DIM
RNN_WIDTH = 1024
RNN_BLOCKS = 8
RNN_BLOCK = RNN_WIDTH // RNN_BLOCKS
CONV_WIDTH = 4
RG_C = 8.0
X_HEADS = 4
X_HEAD_DIM = 128
X_WIDTH = X_HEADS * X_HEAD_DIM
D_FF = 5632
NUM_BUCKETS = 32
MAX_DISTANCE = 1024
LAM_INIT = 0.8 - 0.6 * math.exp(-0.3 * 0)
LOG2E = 1.4426950408889634
NEG_BIG = -1e30
MAX_EXP2_SPREAD = 120.0

LANES = 128
SUBLANES = 8
MXU_DIM = 256
VMEM_LIMIT = 56 * 1024 * 1024

TM_PROJ = 512
TM_IN = 1024
TM_FFN = 1024
SEG = 1024
T_ATT = 512
TR_RNN = 256
TF_FFN = 512


def _far_bucket_distance():
    half = NUM_BUCKETS // 2
    max_exact = half // 2
    n = max_exact
    while max_exact + int(math.log(n / max_exact) / math.log(MAX_DISTANCE / max_exact) * (half - max_exact)) < half - 1:
        n += 1
    return n


FAR_DIST = _far_bucket_distance()
ND_ATT = next(d for d in range(1, 64) if (d - 1) * T_ATT + 1 >= FAR_DIST + 64)
FAR_BUCKET = NUM_BUCKETS // 2 - 1
N_BIAS_TILES = ND_ATT + 2
ATT_UNROLL = 2
Q_BLOCKS = 4
assert Q_BLOCKS % ATT_UNROLL == 0


def _rms(x, eps=EPS):
    return x * lax.rsqrt(jnp.mean(x * x, axis=-1, keepdims=True) + eps)


def _inproj_kernel(x_ref, w_ref, gsum_ref, g_ref, qg_ref, kg_ref, outT_ref, outN_ref, h_scr):
    j = pl.program_id(1)

    @pl.when(j == 0)
    def _():
        h_scr[...] = (_rms(x_ref[...]) * g_ref[...]).astype(BF16)

    acc = jnp.dot(h_scr[...], w_ref[...], preferred_element_type=F32)

    def group_norm(a, gain_row):
        sq = (a * a).astype(BF16)
        parts = [jnp.dot(sq[:, c * MXU_DIM:(c + 1) * MXU_DIM], gsum_ref[...], preferred_element_type=F32)
                 for c in range(SEG // MXU_DIM)]
        ms = jnp.concatenate(parts, axis=-1)
        return a * lax.rsqrt(ms + EPS) * gain_row

    @pl.when(j == 0)
    def _():
        outT_ref[...] = group_norm(acc, qg_ref[...]).T.astype(BF16)

    @pl.when(j == 1)
    def _():
        outT_ref[...] = acc.T.astype(BF16)

    @pl.when(j == 2)
    def _():
        outN_ref[...] = group_norm(acc, kg_ref[...]).astype(BF16)

    @pl.when(j >= 3)
    def _():
        outN_ref[...] = acc.astype(BF16)


def _in_proj(x, g, w, gsum, qg, kg):
    s = x.shape[0]
    nseg = w.shape[1] // SEG
    return pl.pallas_call(
        _inproj_kernel,
        grid=(s // TM_IN, nseg),
        in_specs=[
            pl.BlockSpec((TM_IN, D_MODEL), lambda i, j: (i, 0)),
            pl.BlockSpec((D_MODEL, SEG), lambda i, j: (0, j)),
            pl.BlockSpec((MXU_DIM, MXU_DIM), lambda i, j: (0, 0)),
            pl.BlockSpec((1, D_MODEL), lambda i, j: (0, 0)),
            pl.BlockSpec((1, SEG), lambda i, j: (0, 0)),
            pl.BlockSpec((1, SEG), lambda i, j: (0, 0)),
        ],
        out_specs=[
            pl.BlockSpec((None, SEG, TM_IN), lambda i, j: (jnp.minimum(j, 1), 0, i)),
            pl.BlockSpec((None, TM_IN, SEG), lambda i, j: (jnp.maximum(j - 2, 0), i, 0)),
        ],
        out_shape=[
            jax.ShapeDtypeStruct((2, SEG, s), BF16),
            jax.ShapeDtypeStruct((3, s, SEG), BF16),
        ],
        scratch_shapes=[pltpu.VMEM((TM_IN, D_MODEL), BF16)],
        compiler_params=pltpu.CompilerParams(
            dimension_semantics=("arbitrary", "arbitrary"), vmem_limit_bytes=VMEM_LIMIT),
        name="in_proj",
    )(x, w, gsum, g, qg, kg)


def _bias_kernel(rb_ref, shift_ref, out_ref):
    d = pl.program_id(0) - 1
    t = T_ATT
    width = 2 * t
    half = NUM_BUCKETS // 2

    @pl.when(jnp.logical_and(d >= 0, d < ND_ATT))
    def _():
        c = lax.broadcasted_iota(jnp.int32, (SUBLANES, width), 1)
        rel = jnp.where(c < t, -c, width - c) - d * t
        max_exact = half // 2
        n = jnp.abs(rel)
        large = max_exact + (jnp.log(jnp.maximum(n, 1).astype(F32) / max_exact)
                             / math.log(MAX_DISTANCE / max_exact) * (half - max_exact)).astype(jnp.int32)
        large = jnp.minimum(large, half - 1)
        bucket = jnp.where(rel > 0, half, 0) + jnp.where(n < max_exact, n, large)
        kk = lax.broadcasted_iota(jnp.int32, (t, t), 0)
        qq = lax.broadcasted_iota(jnp.int32, (t, t), 1)
        allowed = (kk >> CHUNK_SHIFT) <= ((qq + d * t) >> CHUNK_SHIFT)

        def head(h, carry):
            vec = jnp.zeros((SUBLANES, width), F32)
            for b in range(NUM_BUCKETS):
                vec = jnp.where(bucket == b, rb_ref[b, h], vec)
            vec = (vec - rb_ref[FAR_BUCKET, h]) * LOG2E - shift_ref[h]
            rows = jnp.broadcast_to(vec[0:1, :], (t, width))
            tile = pltpu.roll(rows, 0, 1, stride=1, stride_axis=0)[:, :t]
            out_ref[h] = jnp.where(allowed, tile, NEG_BIG)
            return carry
        lax.fori_loop(0, DA_HEADS, head, 0)

    @pl.when(jnp.logical_or(d < 0, d >= ND_ATT))
    def _():
        def head(h, carry):
            out_ref[h] = jnp.full((t, t), jnp.where(d < 0, NEG_BIG, -shift_ref[h]), F32)
            return carry
        lax.fori_loop(0, DA_HEADS, head, 0)


def _bias_tiles(rel_bias, shift):
    return pl.pallas_call(
        _bias_kernel,
        grid=(N_BIAS_TILES,),
        in_specs=[pl.BlockSpec(memory_space=pltpu.SMEM), pl.BlockSpec(memory_space=pltpu.SMEM)],
        out_specs=pl.BlockSpec((None, DA_HEADS, T_ATT, T_ATT), lambda d: (d, 0, 0, 0)),
        out_shape=jax.ShapeDtypeStruct((N_BIAS_TILES, DA_HEADS, T_ATT, T_ATT), F32),
        compiler_params=pltpu.CompilerParams(
            dimension_semantics=("arbitrary",), vmem_limit_bytes=VMEM_LIMIT),
        name="bias_tiles",
    )(rel_bias, shift)


def _logit_bound(rel_bias, q_gain, k_gain):
    rb = rel_bias.astype(F32)
    b2 = (rb - rb[FAR_BUCKET]) * LOG2E
    qk = (LOG2E * DA_HEAD_DIM ** 0.5 * (1.0 + 2.0 ** -6)) * jnp.max(jnp.abs(q_gain)) * jnp.max(jnp.abs(k_gain))
    bound = qk + jnp.max(b2, axis=0)
    spread = 2.0 * qk + jnp.max(jnp.max(b2, axis=0) - jnp.min(b2, axis=0))
    bounded = spread <= MAX_EXP2_SPREAD
    return jnp.where(bounded, bound, 0.0).astype(F32), bounded.astype(jnp.int32).reshape(1)


def _attn_kernel(bounded_ref, qT_ref, k_ref, vT_ref, bias_ref, subln_ref, lq1_ref, lk1_ref, lq2_ref, lk2_ref,
                 o_ref, p_scr, acc_scr, l_scr):
    i = pl.program_id(1)
    t = T_ATT

    def block_rows(qb):
        return pl.ds(qb * t if isinstance(qb, int) else pl.multiple_of(qb * t, t), t)

    def q_block(sub, qb):
        q = qT_ref[:, block_rows(qb)]
        row = lax.broadcasted_iota(jnp.int32, q.shape, 0)
        keep = row < DA_HEAD_DIM if sub == 0 else row >= DA_HEAD_DIM
        return jnp.where(keep, q, jnp.zeros_like(q))

    def key_tile(j):
        return k_ref[pl.ds(pl.multiple_of(j * t, t), t), :]

    def value_tile(j):
        return vT_ref[:, pl.ds(pl.multiple_of(j * t, t), t)]

    def bias_tile(j, qb):
        return bias_ref[jnp.clip(Q_BLOCKS * i + qb - j, -1, ND_ATT) + 1]

    def finish(qb, acc1, l1, acc2, l2):
        lam = (jnp.exp(jnp.sum(lq1_ref[...] * lk1_ref[...], axis=-1, keepdims=True))
               - jnp.exp(jnp.sum(lq2_ref[...] * lk2_ref[...], axis=-1, keepdims=True)) + LAM_INIT)
        o = acc1 * (1.0 / l1) - acc2 * (lam / l2)
        ms = jnp.mean(o * o, axis=0, keepdims=True)
        y = o * lax.rsqrt(ms + EPS) * (subln_ref[...] * (1.0 - LAM_INIT))
        o_ref[block_rows(qb), :] = y.T.astype(BF16)

    @pl.when(bounded_ref[0] == 1)
    def _():
        def probs(j, sub, qb, bias_index):
            s = jnp.dot(key_tile(j), q_block(sub, qb), preferred_element_type=F32)
            p = jnp.exp2(s + bias_ref[bias_index])
            l_scr[sub, qb] += jnp.sum(p.reshape(t // SUBLANES, SUBLANES, t), axis=0)
            return p.astype(BF16)

        def weighted(j, sub, qb, p):
            acc_scr[sub, qb] += jnp.dot(value_tile(j), p, preferred_element_type=F32)

        def step(r, consume, produce):
            for u in range(ATT_UNROLL):
                for qb in range(Q_BLOCKS):
                    for sub in range(2):
                        if consume:
                            weighted(ATT_UNROLL * (r - 1) + u, sub, qb, p_scr[sub, u, qb])
                        if produce:
                            j = ATT_UNROLL * r + u
                            p_scr[sub, u, qb] = probs(j, sub, qb, jnp.clip(Q_BLOCKS * i + qb - j, -1, ND_ATT) + 1)

        def full_step(r, carry):
            step(r, True, True)
            return carry

        acc_scr[...] = jnp.zeros(acc_scr.shape, F32)
        l_scr[...] = jnp.zeros(l_scr.shape, F32)

        n_full = (Q_BLOCKS // ATT_UNROLL) * i

        @pl.when(i > 0)
        def _():
            step(0, False, True)
            lax.fori_loop(1, n_full, full_step, 0)
            step(n_full, True, False)

        for u in range(Q_BLOCKS):
            for qb in range(u, Q_BLOCKS):
                for sub in range(2):
                    j = Q_BLOCKS * i + u
                    weighted(j, sub, qb, probs(j, sub, qb, min(qb - u, ND_ATT) + 1))

        for qb in range(Q_BLOCKS):
            finish(qb, acc_scr[0, qb], jnp.sum(l_scr[0, qb], axis=0, keepdims=True),
                   acc_scr[1, qb], jnp.sum(l_scr[1, qb], axis=0, keepdims=True))

    @pl.when(bounded_ref[0] != 1)
    def _():
        def query_block(qb, carry):
            q_sub = (q_block(0, qb), q_block(1, qb))

            def body(j, state):
                kt = key_tile(j)
                vt = value_tile(j)
                b = bias_tile(j, qb)
                new = []
                for sub in range(2):
                    m, l, acc = state[sub]
                    s = jnp.dot(kt, q_sub[sub], preferred_element_type=F32) + b
                    m_new = jnp.maximum(m, jnp.max(s, axis=0, keepdims=True))
                    alpha = jnp.exp2(m - m_new)
                    p = jnp.exp2(s - m_new)
                    l_new = alpha * l + jnp.sum(p, axis=0, keepdims=True)
                    acc_new = alpha * acc + jnp.dot(vt, p.astype(BF16), preferred_element_type=F32)
                    new.append((m_new, l_new, acc_new))
                return tuple(new)

            init1 = (jnp.full((1, t), NEG_BIG, F32), jnp.zeros((1, t), F32), jnp.zeros((DA_V_DIM, t), F32))
            (_, l1, acc1), (_, l2, acc2) = lax.fori_loop(0, Q_BLOCKS * i + qb + 1, body, (init1, init1))
            finish(qb, acc1, l1, acc2, l2)
            return carry

        lax.fori_loop(0, Q_BLOCKS, query_block, 0)


def _diff_attn(bounded, lq1, lk1, lq2, lk2, outT, outN, bias, subln):
    s = outN.shape[1]
    t = T_ATT
    tq = Q_BLOCKS * t
    vec = pl.BlockSpec((1, DA_HEAD_DIM), lambda h, i: (0, 0))
    return pl.pallas_call(
        _attn_kernel,
        grid=(DA_HEADS, s // tq),
        in_specs=[
            pl.BlockSpec(memory_space=pltpu.SMEM),
            pl.BlockSpec((None, DA_V_DIM, tq), lambda h, i: (0, h, i)),
            pl.BlockSpec((None, s, DA_V_DIM), lambda h, i: (0, 0, h)),
            pl.BlockSpec((None, DA_V_DIM, s), lambda h, i: (1, h, 0)),
            pl.BlockSpec((N_BIAS_TILES, None, t, t), lambda h, i: (0, h, 0, 0)),
            pl.BlockSpec((DA_V_DIM, 1), lambda h, i: (0, 0)),
            vec, vec, vec, vec,
        ],
        out_specs=pl.BlockSpec((tq, DA_V_DIM), lambda h, i: (i, h)),
        out_shape=jax.ShapeDtypeStruct((s, ATTN_WIDTH), BF16),
        scratch_shapes=[
            pltpu.VMEM((2, ATT_UNROLL, Q_BLOCKS, t, t), BF16),
            pltpu.VMEM((2, Q_BLOCKS, DA_V_DIM, t), F32),
            pltpu.VMEM((2, Q_BLOCKS, SUBLANES, t), F32),
        ],
        compiler_params=pltpu.CompilerParams(
            dimension_semantics=("arbitrary", "arbitrary"), vmem_limit_bytes=VMEM_LIMIT),
        name="diff_attn",
    )(bounded, outT, outN, outT, bias, subln, lq1, lk1, lq2, lk2)


def _rglru_kernel(xr_ref, yr_ref, cw_ref, cb_ref, wg_ref, gab_ref, gxb_ref, lam_ref, gn_ref, o_ref,
                  xpad, hstate, a_scr, b_scr, h_scr):
    step = pl.program_id(0)
    tr = TR_RNN
    c = RNN_WIDTH

    @pl.when(step == 0)
    def _():
        xpad[0:SUBLANES, :] = jnp.zeros((SUBLANES, c), F32)
        hstate[...] = jnp.zeros((1, c), F32)

    x = xr_ref[...].astype(F32)
    xpad[SUBLANES:SUBLANES + tr, :] = x
    xc = cb_ref[...] + cw_ref[CONV_WIDTH - 1:CONV_WIDTH, :] * x
    for back in range(1, CONV_WIDTH):
        w_row = cw_ref[CONV_WIDTH - 1 - back:CONV_WIDTH - back, :]
        xc = xc + w_row * xpad[SUBLANES - back:SUBLANES - back + tr, :]
    xpad[0:SUBLANES, :] = x[tr - SUBLANES:tr, :]

    xcb = xc.astype(BF16)
    ga_parts, gx_parts = [], []
    for g in range(RNN_BLOCKS):
        gg = jnp.dot(xcb[:, g * RNN_BLOCK:(g + 1) * RNN_BLOCK], wg_ref[g], preferred_element_type=F32)
        ga_parts.append(gg[:, :RNN_BLOCK])
        gx_parts.append(gg[:, RNN_BLOCK:])
    gate_a = jax.nn.sigmoid(jnp.concatenate(ga_parts, axis=-1) + gab_ref[...])
    gate_x = jax.nn.sigmoid(jnp.concatenate(gx_parts, axis=-1) + gxb_ref[...])
    z = -lam_ref[...]
    softplus = jnp.maximum(z, 0.0) + jnp.log1p(jnp.exp(-jnp.abs(z)))
    log_a = -RG_C * gate_a * softplus
    a = jnp.exp(log_a)
    one_m_a2 = 1.0 - a * a
    root = jnp.where(one_m_a2 > 0.0, one_m_a2 * lax.rsqrt(one_m_a2), 0.0)
    u = (xc * gate_x) * root

    groups = tr // SUBLANES
    a3 = a.reshape(groups, SUBLANES, c)
    b3 = u.reshape(groups, SUBLANES, c)
    row = lax.broadcasted_iota(jnp.int32, a3.shape, 1)
    shift = 1
    while shift < SUBLANES:
        a_sh = pltpu.roll(a3, shift, axis=1)
        b_sh = pltpu.roll(b3, shift, axis=1)
        keep = row >= shift
        b3 = jnp.where(keep, a3 * b_sh + b3, b3)
        a3 = jnp.where(keep, a3 * a_sh, a3)
        shift *= 2
    a_scr[...] = a3.reshape(tr, c)
    b_scr[...] = b3.reshape(tr, c)

    def carry_body(g, h):
        r0 = pl.multiple_of(g * SUBLANES, SUBLANES)
        out = a_scr[pl.ds(r0, SUBLANES), :] * h + b_scr[pl.ds(r0, SUBLANES), :]
        h_scr[pl.ds(r0, SUBLANES), :] = out
        return out[SUBLANES - 1:SUBLANES, :]

    hstate[...] = lax.fori_loop(0, groups, carry_body, hstate[...])

    y = yr_ref[...].astype(F32)
    gelu = 0.5 * y * (1.0 + jnp.tanh(math.sqrt(2.0 / math.pi) * (y + 0.044715 * (y * y * y))))
    o_ref[...] = (_rms(h_scr[...] * gelu) * gn_ref[...]).astype(BF16)


def _rglru(outN, conv_w, conv_b, w_gates, ga_b, gx_b, rg_lambda, gn):
    s = outN.shape[1]
    c = RNN_WIDTH
    row = pl.BlockSpec((1, c), lambda t: (0, 0))
    return pl.pallas_call(
        _rglru_kernel,
        grid=(s // TR_RNN,),
        in_specs=[
            pl.BlockSpec((None, TR_RNN, c), lambda t: (1, t, 0)),
            pl.BlockSpec((None, TR_RNN, c), lambda t: (2, t, 0)),
            pl.BlockSpec((CONV_WIDTH, c), lambda t: (0, 0)),
            row,
            pl.BlockSpec((RNN_BLOCKS, RNN_BLOCK, 2 * RNN_BLOCK), lambda t: (0, 0, 0)),
            row, row, row, row,
        ],
        out_specs=pl.BlockSpec((TR_RNN, c), lambda t: (t, 0)),
        out_shape=jax.ShapeDtypeStruct((s, c), BF16),
        scratch_shapes=[
            pltpu.VMEM((TR_RNN + SUBLANES, c), F32),
            pltpu.VMEM((1, c), F32),
            pltpu.VMEM((TR_RNN, c), F32),
            pltpu.VMEM((TR_RNN, c), F32),
            pltpu.VMEM((TR_RNN, c), F32),
        ],
        compiler_params=pltpu.CompilerParams(
            dimension_semantics=("arbitrary",), vmem_limit_bytes=VMEM_LIMIT),
        name="rglru",
    )(outN, outN, conv_w, conv_b, w_gates, ga_b, gx_b, rg_lambda, gn)


def _outproj_kernel(x_ref, oa_ref, or_ref, w_ref, o_ref):
    y = jnp.dot(oa_ref[...], w_ref[0:ATTN_WIDTH, :], preferred_element_type=F32)
    y = y + jnp.dot(or_ref[...], w_ref[ATTN_WIDTH:, :], preferred_element_type=F32)
    o_ref[...] = x_ref[...] + y


def _out_proj(x, oa, orn, w):
    s = x.shape[0]
    tm = TM_PROJ
    return pl.pallas_call(
        _outproj_kernel,
        grid=(s // tm,),
        in_specs=[
            pl.BlockSpec((tm, D_MODEL), lambda i: (i, 0)),
            pl.BlockSpec((tm, ATTN_WIDTH), lambda i: (i, 0)),
            pl.BlockSpec((tm, RNN_WIDTH), lambda i: (i, 0)),
            pl.BlockSpec((ATTN_WIDTH + RNN_WIDTH, D_MODEL), lambda i: (0, 0)),
        ],
        out_specs=pl.BlockSpec((tm, D_MODEL), lambda i: (i, 0)),
        out_shape=jax.ShapeDtypeStruct((s, D_MODEL), F32),
        compiler_params=pltpu.CompilerParams(
            dimension_semantics=("arbitrary",), vmem_limit_bytes=VMEM_LIMIT),
        name="out_proj",
    )(x, oa, orn, w)


def _memkv_kernel(mem_ref, g_ref, wk_ref, wv_ref, kg_ref, kT_ref, v_ref):
    m = (_rms(mem_ref[...]) * g_ref[...]).astype(BF16)
    k = jnp.dot(m, wk_ref[...], preferred_element_type=F32)
    parts = [_rms(k[:, h * X_HEAD_DIM:(h + 1) * X_HEAD_DIM]) for h in range(X_HEADS)]
    kn = jnp.concatenate(parts, axis=-1) * kg_ref[...]
    kT_ref[...] = kn.T.astype(BF16)
    v_ref[...] = jnp.dot(m, wv_ref[...], preferred_element_type=F32).astype(BF16)


def _mem_kv(mem, g, wk, wv, kg):
    return pl.pallas_call(
        _memkv_kernel,
        out_shape=[
            jax.ShapeDtypeStruct((X_WIDTH, MEM_LEN), BF16),
            jax.ShapeDtypeStruct((MEM_LEN, X_WIDTH), BF16),
        ],
        compiler_params=pltpu.CompilerParams(vmem_limit_bytes=VMEM_LIMIT),
        name="mem_kv",
    )(mem, g, wk, wv, kg)


def _xattn_kernel(x_ref, wq_ref, kT_ref, v_ref, wo_ref, g_ref, qg_ref, o_ref):
    x = x_ref[...]
    hx = (_rms(x) * g_ref[...]).astype(BF16)
    q = jnp.dot(hx, wq_ref[...], preferred_element_type=F32)
    outs = []
    for h in range(X_HEADS):
        sl = slice(h * X_HEAD_DIM, (h + 1) * X_HEAD_DIM)
        qn = (_rms(q[:, sl]) * qg_ref[:, sl]).astype(BF16)
        s = jnp.dot(qn, kT_ref[sl, :], preferred_element_type=F32)
        p = jnp.exp2(s - jnp.max(s, axis=-1, keepdims=True))
        l = jnp.sum(p, axis=-1, keepdims=True)
        o = jnp.dot(p.astype(BF16), v_ref[:, sl], preferred_element_type=F32) / l
        outs.append(o.astype(BF16))
    o_all = jnp.concatenate(outs, axis=-1)
    o_ref[...] = x + jnp.dot(o_all, wo_ref[...], preferred_element_type=F32)


def _xattn(x, g, wq, qg, kT, v, wo):
    s = x.shape[0]
    tm = TM_PROJ
    full = lambda shape: pl.BlockSpec(shape, lambda i: tuple(0 for _ in shape))
    return pl.pallas_call(
        _xattn_kernel,
        grid=(s // tm,),
        in_specs=[
            pl.BlockSpec((tm, D_MODEL), lambda i: (i, 0)),
            full((D_MODEL, X_WIDTH)),
            full((X_WIDTH, MEM_LEN)),
            full((MEM_LEN, X_WIDTH)),
            full((X_WIDTH, D_MODEL)),
            full((1, D_MODEL)),
            full((1, X_WIDTH)),
        ],
        out_specs=pl.BlockSpec((tm, D_MODEL), lambda i: (i, 0)),
        out_shape=jax.ShapeDtypeStruct((s, D_MODEL), F32),
        compiler_params=pltpu.CompilerParams(
            dimension_semantics=("arbitrary",), vmem_limit_bytes=VMEM_LIMIT),
        name="xattn",
    )(x, wq, kT, v, wo, g, qg)


def _ffn_kernel(x_ref, wg_ref, wu_ref, wd_ref, g_ref, o_ref, h_scr):
    f = pl.program_id(1)

    @pl.when(f == 0)
    def _():
        x = x_ref[...]
        h_scr[...] = (_rms(x) * g_ref[...]).astype(BF16)
        o_ref[...] = x

    h = h_scr[...]
    gate = jnp.dot(h, wg_ref[...], preferred_element_type=F32)
    up = jnp.dot(h, wu_ref[...], preferred_element_type=F32)
    act = (gate * jax.nn.sigmoid(gate) * up).astype(BF16)
    o_ref[...] += jnp.dot(act, wd_ref[...], preferred_element_type=F32)


def _ffn(x, g, w_gate_up, w_down):
    s = x.shape[0]
    tm = TM_FFN
    nf = D_FF // TF_FFN
    return pl.pallas_call(
        _ffn_kernel,
        grid=(s // tm, nf),
        in_specs=[
            pl.BlockSpec((tm, D_MODEL), lambda i, f: (i, 0)),
            pl.BlockSpec((D_MODEL, TF_FFN), lambda i, f: (0, f)),
            pl.BlockSpec((D_MODEL, TF_FFN), lambda i, f: (0, nf + f)),
            pl.BlockSpec((TF_FFN, D_MODEL), lambda i, f: (f, 0)),
            pl.BlockSpec((1, D_MODEL), lambda i, f: (0, 0)),
        ],
        out_specs=pl.BlockSpec((tm, D_MODEL), lambda i, f: (i, 0)),
        out_shape=jax.ShapeDtypeStruct((s, D_MODEL), F32),
        scratch_shapes=[pltpu.VMEM((tm, D_MODEL), BF16)],
        compiler_params=pltpu.CompilerParams(
            dimension_semantics=("arbitrary", "arbitrary"), vmem_limit_bytes=VMEM_LIMIT),
        name="ffn",
    )(x, w_gate_up, w_gate_up, w_down, g)


def kernel(x, mem, rel_bias, attn_norm, w_in, da_q_norm, da_k_norm, da_lambda_q1, da_lambda_k1, da_lambda_q2, da_lambda_k2, da_subln, conv_w, conv_b, gate_a_w, gate_a_b, gate_x_w, gate_x_b, rg_lambda, rnn_out_norm, w_out, xattn_norm, mem_norm, xq_w, xk_w, xv_w, xq_norm, xk_norm, xo_w, ffn_norm, w_gate_up, w_down):
    assert x.shape == (1, SEQ, D_MODEL) and mem.shape == (1, MEM_LEN, D_MODEL)
    assert w_in.shape == (1, D_MODEL, 3 * ATTN_WIDTH + 2 * RNN_WIDTH)
    assert w_gate_up.shape == (1, D_MODEL, 2 * D_FF) and w_down.shape == (1, D_FF, D_MODEL)
    assert SEQ % (T_ATT * Q_BLOCKS) == 0 and SEQ % TM_PROJ == 0 and SEQ % TM_IN == 0 and SEQ % TM_FFN == 0
    assert SEQ % TR_RNN == 0 and D_FF % TF_FFN == 0
    x2d = x[0]
    row = lambda v: v.reshape(1, -1).astype(F32)

    a = ATTN_WIDTH
    wi = w_in[0]
    w_in_b = jnp.concatenate([wi[:, 0:a], wi[:, 2 * a:3 * a], wi[:, a:2 * a], wi[:, 3 * a:]], axis=1).astype(BF16)
    grp = jnp.arange(MXU_DIM) // DA_HEAD_DIM
    gsum = jnp.where(grp[:, None] == grp[None, :], 1.0 / DA_HEAD_DIM, 0.0).astype(BF16)
    reps = ATTN_WIDTH // DA_HEAD_DIM
    qg = row(jnp.tile(da_q_norm[0], reps)) * (DA_HEAD_DIM ** -0.5 * LOG2E)
    kg = row(jnp.tile(da_k_norm[0], reps))

    outT, outN = _in_proj(x2d, row(attn_norm[0]), w_in_b, gsum, qg, kg)

    shift, bounded = _logit_bound(rel_bias, da_q_norm[0], da_k_norm[0])
    bias = _bias_tiles(rel_bias.astype(F32), shift)
    o_attn = _diff_attn(bounded, row(da_lambda_q1[0]), row(da_lambda_k1[0]), row(da_lambda_q2[0]), row(da_lambda_k2[0]),
                        outT, outN, bias, da_subln[0].reshape(DA_V_DIM, 1).astype(F32))

    w_gates = jnp.concatenate([gate_a_w[0], gate_x_w[0]], axis=-1).astype(BF16)
    o_rnn = _rglru(outN, conv_w[0].astype(F32), row(conv_b[0]), w_gates, row(gate_a_b[0]), row(gate_x_b[0]),
                   row(rg_lambda[0]), row(rnn_out_norm[0]))

    x1 = _out_proj(x2d, o_attn, o_rnn, w_out[0].astype(BF16))

    kT, vm = _mem_kv(mem[0], row(mem_norm[0]), xk_w[0].astype(BF16), xv_w[0].astype(BF16),
                     row(jnp.tile(xk_norm[0], X_HEADS)))
    xqg = row(jnp.tile(xq_norm[0], X_HEADS)) * (X_HEAD_DIM ** -0.5 * LOG2E)
    x2 = _xattn(x1, row(xattn_norm[0]), xq_w[0].astype(BF16), xqg, kT, vm, xo_w[0].astype(BF16))

    x3 = _ffn(x2, row(ffn_norm[0]), w_gate_up[0].astype(BF16), w_down[0].astype(BF16))
    return x3[None]
```

```python
import functools
import math

import jax
import jax.numpy as jnp
from jax import lax
from jax.experimental import pallas as pl
from jax.experimental.pallas import tpu as pltpu

F32 = jnp.float32
BF16 = jnp.bfloat16

D_MODEL = 2048
SEQ = 16384
CHUNK = 64
CHUNK_SHIFT = CHUNK.bit_length() - 1
assert 1 << CHUNK_SHIFT == CHUNK
MEM_LEN = 256
EPS = 1e-6
ATTN_WIDTH = 1024
DA_HEAD_DIM = 64
DA_HEADS = 8
DA_V_DIM = 2 * DA_HEAD_DIM
RNN_WIDTH = 1024
RNN_BLOCKS = 8
RNN_BLOCK = RNN_WIDTH // RNN_BLOCKS
CONV_WIDTH = 4
RG_C = 8.0
X_HEADS = 4
X_HEAD_DIM = 128
X_WIDTH = X_HEADS * X_HEAD_DIM
D_FF = 5632
NUM_BUCKETS = 32
MAX_DISTANCE = 1024
LAM_INIT = 0.8 - 0.6 * math.exp(-0.3 * 0)
LOG2E = 1.4426950408889634
NEG_BIG = -1e30
MAX_EXP2_SPREAD = 120.0

LANES = 128
SUBLANES = 8
MXU_DIM = 256
VMEM_LIMIT = 56 * 1024 * 1024

TM_PROJ = 512
TM_IN = 1024
TM_FFN = 1024
SEG = 1024
T_ATT = 512
TR_RNN = 256
TF_FFN = 512


def _far_bucket_distance():
    half = NUM_BUCKETS // 2
    max_exact = half // 2
    n = max_exact
    while max_exact + int(math.log(n / max_exact) / math.log(MAX_DISTANCE / max_exact) * (half - max_exact)) < half - 1:
        n += 1
    return n


FAR_DIST = _far_bucket_distance()
ND_ATT = next(d for d in range(1, 64) if (d - 1) * T_ATT + 1 >= FAR_DIST + 64)
FAR_BUCKET = NUM_BUCKETS // 2 - 1
N_BIAS_TILES = ND_ATT + 2
ATT_UNROLL = 4
Q_BLOCKS = 4
assert Q_BLOCKS % ATT_UNROLL == 0


def _rms(x, eps=EPS):
    return x * lax.rsqrt(jnp.mean(x * x, axis=-1, keepdims=True) + eps)


def _inproj_kernel(x_ref, w_ref, gsum_ref, g_ref, qg_ref, kg_ref, outT_ref, outN_ref, h_scr):
    j = pl.program_id(1)

    @pl.when(j == 0)
    def _():
        h_scr[...] = (_rms(x_ref[...]) * g_ref[...]).astype(BF16)

    acc = jnp.dot(h_scr[...], w_ref[...], preferred_element_type=F32)

    def group_norm(a, gain_row):
        sq = (a * a).astype(BF16)
        parts = [jnp.dot(sq[:, c * MXU_DIM:(c + 1) * MXU_DIM], gsum_ref[...], preferred_element_type=F32)
                 for c in range(SEG // MXU_DIM)]
        ms = jnp.concatenate(parts, axis=-1)
        return a * lax.rsqrt(ms + EPS) * gain_row

    @pl.when(j == 0)
    def _():
        outT_ref[...] = group_norm(acc, qg_ref[...]).T.astype(BF16)

    @pl.when(j == 1)
    def _():
        outT_ref[...] = acc.T.astype(BF16)

    @pl.when(j == 2)
    def _():
        outN_ref[...] = group_norm(acc, kg_ref[...]).astype(BF16)

    @pl.when(j >= 3)
    def _():
        outN_ref[...] = acc.astype(BF16)


def _in_proj(x, g, w, gsum, qg, kg):
    s = x.shape[0]
    nseg = w.shape[1] // SEG
    return pl.pallas_call(
        _inproj_kernel,
        grid=(s // TM_IN, nseg),
        in_specs=[
            pl.BlockSpec((TM_IN, D_MODEL), lambda i, j: (i, 0)),
            pl.BlockSpec((D_MODEL, SEG), lambda i, j: (0, j)),
            pl.BlockSpec((MXU_DIM, MXU_DIM), lambda i, j: (0, 0)),
            pl.BlockSpec((1, D_MODEL), lambda i, j: (0, 0)),
            pl.BlockSpec((1, SEG), lambda i, j: (0, 0)),
            pl.BlockSpec((1, SEG), lambda i, j: (0, 0)),
        ],
        out_specs=[
            pl.BlockSpec((None, SEG, TM_IN), lambda i, j: (jnp.minimum(j, 1), 0, i)),
            pl.BlockSpec((None, TM_IN, SEG), lambda i, j: (jnp.maximum(j - 2, 0), i, 0)),
        ],
        out_shape=[
            jax.ShapeDtypeStruct((2, SEG, s), BF16),
            jax.ShapeDtypeStruct((3, s, SEG), BF16),
        ],
        scratch_shapes=[pltpu.VMEM((TM_IN, D_MODEL), BF16)],
        compiler_params=pltpu.CompilerParams(
            dimension_semantics=("arbitrary", "arbitrary"), vmem_limit_bytes=VMEM_LIMIT),
        name="in_proj",
    )(x, w, gsum, g, qg, kg)


def _bias_kernel(rb_ref, shift_ref, out_ref):
    d = pl.program_id(0) - 1
    t = T_ATT
    width = 2 * t
    half = NUM_BUCKETS // 2

    @pl.when(jnp.logical_and(d >= 0, d < ND_ATT))
    def _():
        c = lax.broadcasted_iota(jnp.int32, (SUBLANES, width), 1)
        rel = jnp.where(c < t, -c, width - c) - d * t
        max_exact = half // 2
        n = jnp.abs(rel)
        large = max_exact + (jnp.log(jnp.maximum(n, 1).astype(F32) / max_exact)
                             / math.log(MAX_DISTANCE / max_exact) * (half - max_exact)).astype(jnp.int32)
        large = jnp.minimum(large, half - 1)
        bucket = jnp.where(rel > 0, half, 0) + jnp.where(n < max_exact, n, large)
        kk = lax.broadcasted_iota(jnp.int32, (t, t), 0)
        qq = lax.broadcasted_iota(jnp.int32, (t, t), 1)
        allowed = (kk >> CHUNK_SHIFT) <= ((qq + d * t) >> CHUNK_SHIFT)

        def head(h, carry):
            vec = jnp.zeros((SUBLANES, width), F32)
            for b in range(NUM_BUCKETS):
                vec = jnp.where(bucket == b, rb_ref[b, h], vec)
            vec = (vec - rb_ref[FAR_BUCKET, h]) * LOG2E - shift_ref[h]
            rows = jnp.broadcast_to(vec[0:1, :], (t, width))
            tile = pltpu.roll(rows, 0, 1, stride=1, stride_axis=0)[:, :t]
            out_ref[h] = jnp.where(allowed, tile, NEG_BIG)
            return carry
        lax.fori_loop(0, DA_HEADS, head, 0)

    @pl.when(jnp.logical_or(d < 0, d >= ND_ATT))
    def _():
        def head(h, carry):
            out_ref[h] = jnp.full((t, t), jnp.where(d < 0, NEG_BIG, -shift_ref[h]), F32)
            return carry
        lax.fori_loop(0, DA_HEADS, head, 0)


def _bias_tiles(rel_bias, shift):
    return pl.pallas_call(
        _bias_kernel,
        grid=(N_BIAS_TILES,),
        in_specs=[pl.BlockSpec(memory_space=pltpu.SMEM), pl.BlockSpec(memory_space=pltpu.SMEM)],
        out_specs=pl.BlockSpec((None, DA_HEADS, T_ATT, T_ATT), lambda d: (d, 0, 0, 0)),
        out_shape=jax.ShapeDtypeStruct((N_BIAS_TILES, DA_HEADS, T_ATT, T_ATT), F32),
        compiler_params=pltpu.CompilerParams(
            dimension_semantics=("arbitrary",), vmem_limit_bytes=VMEM_LIMIT),
        name="bias_tiles",
    )(rel_bias, shift)


def _logit_bound(rel_bias, q_gain, k_gain):
    rb = rel_bias.astype(F32)
    b2 = (rb - rb[FAR_BUCKET]) * LOG2E
    qk = (LOG2E * DA_HEAD_DIM ** 0.5 * (1.0 + 2.0 ** -6)) * jnp.max(jnp.abs(q_gain)) * jnp.max(jnp.abs(k_gain))
    bound = qk + jnp.max(b2, axis=0)
    spread = 2.0 * qk + jnp.max(jnp.max(b2, axis=0) - jnp.min(b2, axis=0))
    bounded = spread <= MAX_EXP2_SPREAD
    return jnp.where(bounded, bound, 0.0).astype(F32), bounded.astype(jnp.int32).reshape(1)


def _attn_kernel(bounded_ref, qT_ref, k_ref, vT_ref, bias_ref, subln_ref, lq1_ref, lk1_ref, lq2_ref, lk2_ref,
                 o_ref, p_scr, acc_scr, l_scr):
    i = pl.program_id(1)
    t = T_ATT

    def block_rows(qb):
        return pl.ds(qb * t if isinstance(qb, int) else pl.multiple_of(qb * t, t), t)

    def q_block(sub, qb):
        q = qT_ref[:, block_rows(qb)]
        row = lax.broadcasted_iota(jnp.int32, q.shape, 0)
        keep = row < DA_HEAD_DIM if sub == 0 else row >= DA_HEAD_DIM
        return jnp.where(keep, q, jnp.zeros_like(q))

    def key_tile(j):
        return k_ref[pl.ds(pl.multiple_of(j * t, t), t), :]

    def value_tile(j):
        return vT_ref[:, pl.ds(pl.multiple_of(j * t, t), t)]

    def bias_tile(j, qb):
        return bias_ref[jnp.clip(Q_BLOCKS * i + qb - j, -1, ND_ATT) + 1]

    def finish(qb, acc1, l1, acc2, l2):
        lam = (jnp.exp(jnp.sum(lq1_ref[...] * lk1_ref[...], axis=-1, keepdims=True))
               - jnp.exp(jnp.sum(lq2_ref[...] * lk2_ref[...], axis=-1, keepdims=True)) + LAM_INIT)
        o = acc1 * (1.0 / l1) - acc2 * (lam / l2)
        ms = jnp.mean(o * o, axis=0, keepdims=True)
        y = o * lax.rsqrt(ms + EPS) * (subln_ref[...] * (1.0 - LAM_INIT))
        o_ref[block_rows(qb), :] = y.T.astype(BF16)

    @pl.when(bounded_ref[0] == 1)
    def _():
        def probs(j, sub, qb, bias_index):
            s = jnp.dot(key_tile(j), q_block(sub, qb), preferred_element_type=F32)
            p = jnp.exp2(s + bias_ref[bias_index])
            l_scr[sub, qb] += jnp.sum(p.reshape(t // SUBLANES, SUBLANES, t), axis=0)
            return p.astype(BF16)

        def weighted(j, sub, qb, p):
            acc_scr[sub, qb] += jnp.dot(value_tile(j), p, preferred_element_type=F32)

        def step(r, consume, produce):
            for u in range(ATT_UNROLL):
                for qb in range(Q_BLOCKS):
                    for sub in range(2):
                        if consume:
                            weighted(ATT_UNROLL * (r - 1) + u, sub, qb, p_scr[sub, u, qb])
                        if produce:
                            j = ATT_UNROLL * r + u
                            p_scr[sub, u, qb] = probs(j, sub, qb, jnp.clip(Q_BLOCKS * i + qb - j, -1, ND_ATT) + 1)

        def full_step(r, carry):
            step(r, True, True)
            return carry

        acc_scr[...] = jnp.zeros(acc_scr.shape, F32)
        l_scr[...] = jnp.zeros(l_scr.shape, F32)

        n_full = (Q_BLOCKS // ATT_UNROLL) * i

        @pl.when(i > 0)
        def _():
            step(0, False, True)
            lax.fori_loop(1, n_full, full_step, 0)
            step(n_full, True, False)

        for u in range(Q_BLOCKS):
            for qb in range(u, Q_BLOCKS):
                for sub in range(2):
                    j = Q_BLOCKS * i + u
                    weighted(j, sub, qb, probs(j, sub, qb, min(qb - u, ND_ATT) + 1))

        for qb in range(Q_BLOCKS):
            finish(qb, acc_scr[0, qb], jnp.sum(l_scr[0, qb], axis=0, keepdims=True),
                   acc_scr[1, qb], jnp.sum(l_scr[1, qb], axis=0, keepdims=True))

    @pl.when(bounded_ref[0] != 1)
    def _():
        def query_block(qb, carry):
            q_sub = (q_block(0, qb), q_block(1, qb))

            def body(j, state):
                kt = key_tile(j)
                vt = value_tile(j)
                b = bias_tile(j, qb)
                new = []
                for sub in range(2):
                    m, l, acc = state[sub]
                    s = jnp.dot(kt, q_sub[sub], preferred_element_type=F32) + b
                    m_new = jnp.maximum(m, jnp.max(s, axis=0, keepdims=True))
                    alpha = jnp.exp2(m - m_new)
                    p = jnp.exp2(s - m_new)
                    l_new = alpha * l + jnp.sum(p, axis=0, keepdims=True)
                    acc_new = alpha * acc + jnp.dot(vt, p.astype(BF16), preferred_element_type=F32)
                    new.append((m_new, l_new, acc_new))
                return tuple(new)

            init1 = (jnp.full((1, t), NEG_BIG, F32), jnp.zeros((1, t), F32), jnp.zeros((DA_V_DIM, t), F32))
            (_, l1, acc1), (_, l2, acc2) = lax.fori_loop(0, Q_BLOCKS * i + qb + 1, body, (init1, init1))
            finish(qb, acc1, l1, acc2, l2)
            return carry

        lax.fori_loop(0, Q_BLOCKS, query_block, 0)


def _diff_attn(bounded, lq1, lk1, lq2, lk2, outT, outN, bias, subln):
    s = outN.shape[1]
    t = T_ATT
    tq = Q_BLOCKS * t
    vec = pl.BlockSpec((1, DA_HEAD_DIM), lambda h, i: (0, 0))
    return pl.pallas_call(
        _attn_kernel,
        grid=(DA_HEADS, s // tq),
        in_specs=[
            pl.BlockSpec(memory_space=pltpu.SMEM),
            pl.BlockSpec((None, DA_V_DIM, tq), lambda h, i: (0, h, i)),
            pl.BlockSpec((None, s, DA_V_DIM), lambda h, i: (0, 0, h)),
            pl.BlockSpec((None, DA_V_DIM, s), lambda h, i: (1, h, 0)),
            pl.BlockSpec((N_BIAS_TILES, None, t, t), lambda h, i: (0, h, 0, 0)),
            pl.BlockSpec((DA_V_DIM, 1), lambda h, i: (0, 0)),
            vec, vec, vec, vec,
        ],
        out_specs=pl.BlockSpec((tq, DA_V_DIM), lambda h, i: (i, h)),
        out_shape=jax.ShapeDtypeStruct((s, ATTN_WIDTH), BF16),
        scratch_shapes=[
            pltpu.VMEM((2, ATT_UNROLL, Q_BLOCKS, t, t), BF16),
            pltpu.VMEM((2, Q_BLOCKS, DA_V_DIM, t), F32),
            pltpu.VMEM((2, Q_BLOCKS, SUBLANES, t), F32),
        ],
        compiler_params=pltpu.CompilerParams(
            dimension_semantics=("arbitrary", "arbitrary"), vmem_limit_bytes=VMEM_LIMIT),
        name="diff_attn",
    )(bounded, outT, outN, outT, bias, subln, lq1, lk1, lq2, lk2)


def _rglru_kernel(xr_ref, yr_ref, cw_ref, cb_ref, wg_ref, gab_ref, gxb_ref, lam_ref, gn_ref, o_ref,
                  xpad, hstate, a_scr, b_scr, h_scr):
    step = pl.program_id(0)
    tr = TR_RNN
    c = RNN_WIDTH

    @pl.when(step == 0)
    def _():
        xpad[0:SUBLANES, :] = jnp.zeros((SUBLANES, c), F32)
        hstate[...] = jnp.zeros((1, c), F32)

    x = xr_ref[...].astype(F32)
    xpad[SUBLANES:SUBLANES + tr, :] = x
    xc = cb_ref[...] + cw_ref[CONV_WIDTH - 1:CONV_WIDTH, :] * x
    for back in range(1, CONV_WIDTH):
        w_row = cw_ref[CONV_WIDTH - 1 - back:CONV_WIDTH - back, :]
        xc = xc + w_row * xpad[SUBLANES - back:SUBLANES - back + tr, :]
    xpad[0:SUBLANES, :] = x[tr - SUBLANES:tr, :]

    xcb = xc.astype(BF16)
    ga_parts, gx_parts = [], []
    for g in range(RNN_BLOCKS):
        gg = jnp.dot(xcb[:, g * RNN_BLOCK:(g + 1) * RNN_BLOCK], wg_ref[g], preferred_element_type=F32)
        ga_parts.append(gg[:, :RNN_BLOCK])
        gx_parts.append(gg[:, RNN_BLOCK:])
    gate_a = jax.nn.sigmoid(jnp.concatenate(ga_parts, axis=-1) + gab_ref[...])
    gate_x = jax.nn.sigmoid(jnp.concatenate(gx_parts, axis=-1) + gxb_ref[...])
    z = -lam_ref[...]
    softplus = jnp.maximum(z, 0.0) + jnp.log1p(jnp.exp(-jnp.abs(z)))
    log_a = -RG_C * gate_a * softplus
    a = jnp.exp(log_a)
    one_m_a2 = 1.0 - a * a
    root = jnp.where(one_m_a2 > 0.0, one_m_a2 * lax.rsqrt(one_m_a2), 0.0)
    u = (xc * gate_x) * root

    groups = tr // SUBLANES
    a3 = a.reshape(groups, SUBLANES, c)
    b3 = u.reshape(groups, SUBLANES, c)
    row = lax.broadcasted_iota(jnp.int32, a3.shape, 1)
    shift = 1
    while shift < SUBLANES:
        a_sh = pltpu.roll(a3, shift, axis=1)
        b_sh = pltpu.roll(b3, shift, axis=1)
        keep = row >= shift
        b3 = jnp.where(keep, a3 * b_sh + b3, b3)
        a3 = jnp.where(keep, a3 * a_sh, a3)
        shift *= 2
    a_scr[...] = a3.reshape(tr, c)
    b_scr[...] = b3.reshape(tr, c)

    def carry_body(g, h):
        r0 = pl.multiple_of(g * SUBLANES, SUBLANES)
        out = a_scr[pl.ds(r0, SUBLANES), :] * h + b_scr[pl.ds(r0, SUBLANES), :]
        h_scr[pl.ds(r0, SUBLANES), :] = out
        return out[SUBLANES - 1:SUBLANES, :]

    hstate[...] = lax.fori_loop(0, groups, carry_body, hstate[...])

    y = yr_ref[...].astype(F32)
    gelu = 0.5 * y * (1.0 + jnp.tanh(math.sqrt(2.0 / math.pi) * (y + 0.044715 * (y * y * y))))
    o_ref[...] = (_rms(h_scr[...] * gelu) * gn_ref[...]).astype(BF16)


def _rglru(outN, conv_w, conv_b, w_gates, ga_b, gx_b, rg_lambda, gn):
    s = outN.shape[1]
    c = RNN_WIDTH
    row = pl.BlockSpec((1, c), lambda t: (0, 0))
    return pl.pallas_call(
        _rglru_kernel,
        grid=(s // TR_RNN,),
        in_specs=[
            pl.BlockSpec((None, TR_RNN, c), lambda t: (1, t, 0)),
            pl.BlockSpec((None, TR_RNN, c), lambda t: (2, t, 0)),
            pl.BlockSpec((CONV_WIDTH, c), lambda t: (0, 0)),
            row,
            pl.BlockSpec((RNN_BLOCKS, RNN_BLOCK, 2 * RNN_BLOCK), lambda t: (0, 0, 0)),
            row, row, row, row,
        ],
        out_specs=pl.BlockSpec((TR_RNN, c), lambda t: (t, 0)),
        out_shape=jax.ShapeDtypeStruct((s, c), BF16),
        scratch_shapes=[
            pltpu.VMEM((TR_RNN + SUBLANES, c), F32),
            pltpu.VMEM((1, c), F32),
            pltpu.VMEM((TR_RNN, c), F32),
            pltpu.VMEM((TR_RNN, c), F32),
            pltpu.VMEM((TR_RNN, c), F32),
        ],
        compiler_params=pltpu.CompilerParams(
            dimension_semantics=("arbitrary",), vmem_limit_bytes=VMEM_LIMIT),
        name="rglru",
    )(outN, outN, conv_w, conv_b, w_gates, ga_b, gx_b, rg_lambda, gn)


def _outproj_kernel(x_ref, oa_ref, or_ref, w_ref, o_ref):
    y = jnp.dot(oa_ref[...], w_ref[0:ATTN_WIDTH, :], preferred_element_type=F32)
    y = y + jnp.dot(or_ref[...], w_ref[ATTN_WIDTH:, :], preferred_element_type=F32)
    o_ref[...] = x_ref[...] + y


def _out_proj(x, oa, orn, w):
    s = x.shape[0]
    tm = TM_PROJ
    return pl.pallas_call(
        _outproj_kernel,
        grid=(s // tm,),
        in_specs=[
            pl.BlockSpec((tm, D_MODEL), lambda i: (i, 0)),
            pl.BlockSpec((tm, ATTN_WIDTH), lambda i: (i, 0)),
            pl.BlockSpec((tm, RNN_WIDTH), lambda i: (i, 0)),
            pl.BlockSpec((ATTN_WIDTH + RNN_WIDTH, D_MODEL), lambda i: (0, 0)),
        ],
        out_specs=pl.BlockSpec((tm, D_MODEL), lambda i: (i, 0)),
        out_shape=jax.ShapeDtypeStruct((s, D_MODEL), F32),
        compiler_params=pltpu.CompilerParams(
            dimension_semantics=("arbitrary",), vmem_limit_bytes=VMEM_LIMIT),
        name="out_proj",
    )(x, oa, orn, w)


def _memkv_kernel(mem_ref, g_ref, wk_ref, wv_ref, kg_ref, kT_ref, v_ref):
    m = (_rms(mem_ref[...]) * g_ref[...]).astype(BF16)
    k = jnp.dot(m, wk_ref[...], preferred_element_type=F32)
    parts = [_rms(k[:, h * X_HEAD_DIM:(h + 1) * X_HEAD_DIM]) for h in range(X_HEADS)]
    kn = jnp.concatenate(parts, axis=-1) * kg_ref[...]
    kT_ref[...] = kn.T.astype(BF16)
    v_ref[...] = jnp.dot(m, wv_ref[...], preferred_element_type=F32).astype(BF16)


def _mem_kv(mem, g, wk, wv, kg):
    return pl.pallas_call(
        _memkv_kernel,
        out_shape=[
            jax.ShapeDtypeStruct((X_WIDTH, MEM_LEN), BF16),
            jax.ShapeDtypeStruct((MEM_LEN, X_WIDTH), BF16),
        ],
        compiler_params=pltpu.CompilerParams(vmem_limit_bytes=VMEM_LIMIT),
        name="mem_kv",
    )(mem, g, wk, wv, kg)


def _xattn_kernel(x_ref, wq_ref, kT_ref, v_ref, wo_ref, g_ref, qg_ref, o_ref):
    x = x_ref[...]
    hx = (_rms(x) * g_ref[...]).astype(BF16)
    q = jnp.dot(hx, wq_ref[...], preferred_element_type=F32)
    outs = []
    for h in range(X_HEADS):
        sl = slice(h * X_HEAD_DIM, (h + 1) * X_HEAD_DIM)
        qn = (_rms(q[:, sl]) * qg_ref[:, sl]).astype(BF16)
        s = jnp.dot(qn, kT_ref[sl, :], preferred_element_type=F32)
        p = jnp.exp2(s - jnp.max(s, axis=-1, keepdims=True))
        l = jnp.sum(p, axis=-1, keepdims=True)
        o = jnp.dot(p.astype(BF16), v_ref[:, sl], preferred_element_type=F32) / l
        outs.append(o.astype(BF16))
    o_all = jnp.concatenate(outs, axis=-1)
    o_ref[...] = x + jnp.dot(o_all, wo_ref[...], preferred_element_type=F32)


def _xattn(x, g, wq, qg, kT, v, wo):
    s = x.shape[0]
    tm = TM_PROJ
    full = lambda shape: pl.BlockSpec(shape, lambda i: tuple(0 for _ in shape))
    return pl.pallas_call(
        _xattn_kernel,
        grid=(s // tm,),
        in_specs=[
            pl.BlockSpec((tm, D_MODEL), lambda i: (i, 0)),
            full((D_MODEL, X_WIDTH)),
            full((X_WIDTH, MEM_LEN)),
            full((MEM_LEN, X_WIDTH)),
            full((X_WIDTH, D_MODEL)),
            full((1, D_MODEL)),
            full((1, X_WIDTH)),
        ],
        out_specs=pl.BlockSpec((tm, D_MODEL), lambda i: (i, 0)),
        out_shape=jax.ShapeDtypeStruct((s, D_MODEL), F32),
        compiler_params=pltpu.CompilerParams(
            dimension_semantics=("arbitrary",), vmem_limit_bytes=VMEM_LIMIT),
        name="xattn",
    )(x, wq, kT, v, wo, g, qg)


def _ffn_kernel(x_ref, wg_ref, wu_ref, wd_ref, g_ref, o_ref, h_scr):
    f = pl.program_id(1)

    @pl.when(f == 0)
    def _():
        x = x_ref[...]
        h_scr[...] = (_rms(x) * g_ref[...]).astype(BF16)
        o_ref[...] = x

    h = h_scr[...]
    gate = jnp.dot(h, wg_ref[...], preferred_element_type=F32)
    up = jnp.dot(h, wu_ref[...], preferred_element_type=F32)
    act = (gate * jax.nn.sigmoid(gate) * up).astype(BF16)
    o_ref[...] += jnp.dot(act, wd_ref[...], preferred_element_type=F32)


def _ffn(x, g, w_gate_up, w_down):
    s = x.shape[0]
    tm = TM_FFN
    nf = D_FF // TF_FFN
    return pl.pallas_call(
        _ffn_kernel,
        grid=(s // tm, nf),
        in_specs=[
            pl.BlockSpec((tm, D_MODEL), lambda i, f: (i, 0)),
            pl.BlockSpec((D_MODEL, TF_FFN), lambda i, f: (0, f)),
            pl.BlockSpec((D_MODEL, TF_FFN), lambda i, f: (0, nf + f)),
            pl.BlockSpec((TF_FFN, D_MODEL), lambda i, f: (f, 0)),
            pl.BlockSpec((1, D_MODEL), lambda i, f: (0, 0)),
        ],
        out_specs=pl.BlockSpec((tm, D_MODEL), lambda i, f: (i, 0)),
        out_shape=jax.ShapeDtypeStruct((s, D_MODEL), F32),
        scratch_shapes=[pltpu.VMEM((tm, D_MODEL), BF16)],
        compiler_params=pltpu.CompilerParams(
            dimension_semantics=("arbitrary", "arbitrary"), vmem_limit_bytes=VMEM_LIMIT),
        name="ffn",
    )(x, w_gate_up, w_gate_up, w_down, g)


def kernel(x, mem, rel_bias, attn_norm, w_in, da_q_norm, da_k_norm, da_lambda_q1, da_lambda_k1, da_lambda_q2, da_lambda_k2, da_subln, conv_w, conv_b, gate_a_w, gate_a_b, gate_x_w, gate_x_b, rg_lambda, rnn_out_norm, w_out, xattn_norm, mem_norm, xq_w, xk_w, xv_w, xq_norm, xk_norm, xo_w, ffn_norm, w_gate_up, w_down):
    assert x.shape == (1, SEQ, D_MODEL) and mem.shape == (1, MEM_LEN, D_MODEL)
    assert w_in.shape == (1, D_MODEL, 3 * ATTN_WIDTH + 2 * RNN_WIDTH)
    assert w_gate_up.shape == (1, D_MODEL, 2 * D_FF) and w_down.shape == (1, D_FF, D_MODEL)
    assert SEQ % (T_ATT * Q_BLOCKS) == 0 and SEQ % TM_PROJ == 0 and SEQ % TM_IN == 0 and SEQ % TM_FFN == 0
    assert SEQ % TR_RNN == 0 and D_FF % TF_FFN == 0
    x2d = x[0]
    row = lambda v: v.reshape(1, -1).astype(F32)

    a = ATTN_WIDTH
    wi = w_in[0]
    w_in_b = jnp.concatenate([wi[:, 0:a], wi[:, 2 * a:3 * a], wi[:, a:2 * a], wi[:, 3 * a:]], axis=1).astype(BF16)
    grp = jnp.arange(MXU_DIM) // DA_HEAD_DIM
    gsum = jnp.where(grp[:, None] == grp[None, :], 1.0 / DA_HEAD_DIM, 0.0).astype(BF16)
    reps = ATTN_WIDTH // DA_HEAD_DIM
    qg = row(jnp.tile(da_q_norm[0], reps)) * (DA_HEAD_DIM ** -0.5 * LOG2E)
    kg = row(jnp.tile(da_k_norm[0], reps))

    outT, outN = _in_proj(x2d, row(attn_norm[0]), w_in_b, gsum, qg, kg)

    shift, bounded = _logit_bound(rel_bias, da_q_norm[0], da_k_norm[0])
    bias = _bias_tiles(rel_bias.astype(F32), shift)
    o_attn = _diff_attn(bounded, row(da_lambda_q1[0]), row(da_lambda_k1[0]), row(da_lambda_q2[0]), row(da_lambda_k2[0]),
                        outT, outN, bias, da_subln[0].reshape(DA_V_DIM, 1).astype(F32))

    w_gates = jnp.concatenate([gate_a_w[0], gate_x_w[0]], axis=-1).astype(BF16)
    o_rnn = _rglru(outN, conv_w[0].astype(F32), row(conv_b[0]), w_gates, row(gate_a_b[0]), row(gate_x_b[0]),
                   row(rg_lambda[0]), row(rnn_out_norm[0]))

    x1 = _out_proj(x2d, o_attn, o_rnn, w_out[0].astype(BF16))

    kT, vm = _mem_kv(mem[0], row(mem_norm[0]), xk_w[0].astype(BF16), xv_w[0].astype(BF16),
                     row(jnp.tile(xk_norm[0], X_HEADS)))
    xqg = row(jnp.tile(xq_norm[0], X_HEADS)) * (X_HEAD_DIM ** -0.5 * LOG2E)
    x2 = _xattn(x1, row(xattn_norm[0]), xq_w[0].astype(BF16), xqg, kT, vm, xo_w[0].astype(BF16))

    x3 = _ffn(x2, row(ffn_norm[0]), w_gate_up[0].astype(BF16), w_down[0].astype(BF16))
    return x3[None]
```

```python
import functools
import math

import jax
import jax.numpy as jnp
from jax import lax
from jax.experimental import pallas as pl
from jax.experimental.pallas import tpu as pltpu

F32 = jnp.float32
BF16 = jnp.bfloat16

D_MODEL = 2048
SEQ = 16384
CHUNK = 64
CHUNK_SHIFT = CHUNK.bit_length() - 1
assert 1 << CHUNK_SHIFT == CHUNK
MEM_LEN = 256
EPS = 1e-6
ATTN_WIDTH = 1024
DA_HEAD_DIM = 64
DA_HEADS = 8
DA_V_DIM = 2 * DA_HEAD_DIM
RNN_WIDTH = 1024
RNN_BLOCKS = 8
RNN_BLOCK = RNN_WIDTH // RNN_BLOCKS
CONV_WIDTH = 4
RG_C = 8.0
X_HEADS = 4
X_HEAD_DIM = 128
X_WIDTH = X_HEADS * X_HEAD_DIM
D_FF = 5632
NUM_BUCKETS = 32
MAX_DISTANCE = 1024
LAM_INIT = 0.8 - 0.6 * math.exp(-0.3 * 0)
LOG2E = 1.4426950408889634
NEG_BIG = -1e30
MAX_EXP2_SPREAD = 120.0

LANES = 128
SUBLANES = 8
MXU_DIM = 256
VMEM_LIMIT = 56 * 1024 * 1024

TM_PROJ = 512
TM_IN = 512
TM_FFN = 1024
SEG = 1024
N_SEG = 5
T_ATT = 512
TR_RNN = 256
TF_FFN = 512


def _far_bucket_distance():
    half = NUM_BUCKETS // 2
    max_exact = half // 2
    n = max_exact
    while max_exact + int(math.log(n / max_exact) / math.log(MAX_DISTANCE / max_exact) * (half - max_exact)) < half - 1:
        n += 1
    return n


FAR_DIST = _far_bucket_distance()
ND_ATT = next(d for d in range(1, 64) if (d - 1) * T_ATT + 1 >= FAR_DIST + 64)
FAR_BUCKET = NUM_BUCKETS // 2 - 1
N_BIAS_TILES = ND_ATT + 2
ATT_UNROLL = 4
Q_BLOCKS = 4
assert Q_BLOCKS % ATT_UNROLL == 0


def _rms(x, eps=EPS):
    return x * lax.rsqrt(jnp.mean(x * x, axis=-1, keepdims=True) + eps)


def _inproj_kernel(x_ref, w_ref, gsum_ref, g_ref, qg_ref, kg_ref, outT_ref, outN_ref):
    h = (_rms(x_ref[...]) * g_ref[...]).astype(BF16)

    def group_norm(a, gain_row):
        sq = (a * a).astype(BF16)
        parts = [jnp.dot(sq[:, c * MXU_DIM:(c + 1) * MXU_DIM], gsum_ref[...], preferred_element_type=F32)
                 for c in range(SEG // MXU_DIM)]
        ms = jnp.concatenate(parts, axis=-1)
        return a * lax.rsqrt(ms + EPS) * gain_row

    for seg in range(N_SEG):
        acc = jnp.dot(h, w_ref[:, seg * SEG:(seg + 1) * SEG], preferred_element_type=F32)
        if seg == 0:
            outT_ref[0] = group_norm(acc, qg_ref[...]).T.astype(BF16)
        elif seg == 1:
            outT_ref[1] = acc.T.astype(BF16)
        elif seg == 2:
            outN_ref[0] = group_norm(acc, kg_ref[...]).astype(BF16)
        else:
            outN_ref[seg - 2] = acc.astype(BF16)


def _in_proj(x, g, w, gsum, qg, kg):
    s = x.shape[0]
    assert w.shape[1] == N_SEG * SEG
    return pl.pallas_call(
        _inproj_kernel,
        grid=(s // TM_IN,),
        in_specs=[
            pl.BlockSpec((TM_IN, D_MODEL), lambda i: (i, 0)),
            pl.BlockSpec((D_MODEL, N_SEG * SEG), lambda i: (0, 0), pipeline_mode=pl.Buffered(1)),
            pl.BlockSpec((MXU_DIM, MXU_DIM), lambda i: (0, 0)),
            pl.BlockSpec((1, D_MODEL), lambda i: (0, 0)),
            pl.BlockSpec((1, SEG), lambda i: (0, 0)),
            pl.BlockSpec((1, SEG), lambda i: (0, 0)),
        ],
        out_specs=[
            pl.BlockSpec((2, SEG, TM_IN), lambda i: (0, 0, i)),
            pl.BlockSpec((N_SEG - 2, TM_IN, SEG), lambda i: (0, i, 0)),
        ],
        out_shape=[
            jax.ShapeDtypeStruct((2, SEG, s), BF16),
            jax.ShapeDtypeStruct((N_SEG - 2, s, SEG), BF16),
        ],
        compiler_params=pltpu.CompilerParams(
            dimension_semantics=("arbitrary",), vmem_limit_bytes=VMEM_LIMIT),
        name="in_proj",
    )(x, w, gsum, g, qg, kg)


def _bias_kernel(rb_ref, shift_ref, out_ref):
    d = pl.program_id(0) - 1
    t = T_ATT
    width = 2 * t
    half = NUM_BUCKETS // 2

    @pl.when(jnp.logical_and(d >= 0, d < ND_ATT))
    def _():
        c = lax.broadcasted_iota(jnp.int32, (SUBLANES, width), 1)
        rel = jnp.where(c < t, -c, width - c) - d * t
        max_exact = half // 2
        n = jnp.abs(rel)
        large = max_exact + (jnp.log(jnp.maximum(n, 1).astype(F32) / max_exact)
                             / math.log(MAX_DISTANCE / max_exact) * (half - max_exact)).astype(jnp.int32)
        large = jnp.minimum(large, half - 1)
        bucket = jnp.where(rel > 0, half, 0) + jnp.where(n < max_exact, n, large)
        kk = lax.broadcasted_iota(jnp.int32, (t, t), 0)
        qq = lax.broadcasted_iota(jnp.int32, (t, t), 1)
        allowed = (kk >> CHUNK_SHIFT) <= ((qq + d * t) >> CHUNK_SHIFT)

        def head(h, carry):
            vec = jnp.zeros((SUBLANES, width), F32)
            for b in range(NUM_BUCKETS):
                vec = jnp.where(bucket == b, rb_ref[b, h], vec)
            vec = (vec - rb_ref[FAR_BUCKET, h]) * LOG2E - shift_ref[h]
            rows = jnp.broadcast_to(vec[0:1, :], (t, width))
            tile = pltpu.roll(rows, 0, 1, stride=1, stride_axis=0)[:, :t]
            out_ref[h] = jnp.where(allowed, tile, NEG_BIG)
            return carry
        lax.fori_loop(0, DA_HEADS, head, 0)

    @pl.when(jnp.logical_or(d < 0, d >= ND_ATT))
    def _():
        def head(h, carry):
            out_ref[h] = jnp.full((t, t), jnp.where(d < 0, NEG_BIG, -shift_ref[h]), F32)
            return carry
        lax.fori_loop(0, DA_HEADS, head, 0)


def _bias_tiles(rel_bias, shift):
    return pl.pallas_call(
        _bias_kernel,
        grid=(N_BIAS_TILES,),
        in_specs=[pl.BlockSpec(memory_space=pltpu.SMEM), pl.BlockSpec(memory_space=pltpu.SMEM)],
        out_specs=pl.BlockSpec((None, DA_HEADS, T_ATT, T_ATT), lambda d: (d, 0, 0, 0)),
        out_shape=jax.ShapeDtypeStruct((N_BIAS_TILES, DA_HEADS, T_ATT, T_ATT), F32),
        compiler_params=pltpu.CompilerParams(
            dimension_semantics=("arbitrary",), vmem_limit_bytes=VMEM_LIMIT),
        name="bias_tiles",
    )(rel_bias, shift)


def _logit_bound(rel_bias, q_gain, k_gain):
    rb = rel_bias.astype(F32)
    b2 = (rb - rb[FAR_BUCKET]) * LOG2E
    qk = (LOG2E * DA_HEAD_DIM ** 0.5 * (1.0 + 2.0 ** -6)) * jnp.max(jnp.abs(q_gain)) * jnp.max(jnp.abs(k_gain))
    bound = qk + jnp.max(b2, axis=0)
    spread = 2.0 * qk + jnp.max(jnp.max(b2, axis=0) - jnp.min(b2, axis=0))
    bounded = spread <= MAX_EXP2_SPREAD
    return jnp.where(bounded, bound, 0.0).astype(F32), bounded.astype(jnp.int32).reshape(1)


def _attn_kernel(bounded_ref, qT_ref, k_ref, vT_ref, bias_ref, subln_ref, lq1_ref, lk1_ref, lq2_ref, lk2_ref,
                 o_ref, p_scr, acc_scr, l_scr):
    i = pl.program_id(1)
    t = T_ATT

    def block_rows(qb):
        return pl.ds(qb * t if isinstance(qb, int) else pl.multiple_of(qb * t, t), t)

    def q_block(sub, qb):
        q = qT_ref[:, block_rows(qb)]
        row = lax.broadcasted_iota(jnp.int32, q.shape, 0)
        keep = row < DA_HEAD_DIM if sub == 0 else row >= DA_HEAD_DIM
        return jnp.where(keep, q, jnp.zeros_like(q))

    def key_tile(j):
        return k_ref[pl.ds(pl.multiple_of(j * t, t), t), :]

    def value_tile(j):
        return vT_ref[:, pl.ds(pl.multiple_of(j * t, t), t)]

    def bias_tile(j, qb):
        return bias_ref[jnp.clip(Q_BLOCKS * i + qb - j, -1, ND_ATT) + 1]

    def finish(qb, acc1, l1, acc2, l2):
        lam = (jnp.exp(jnp.sum(lq1_ref[...] * lk1_ref[...], axis=-1, keepdims=True))
               - jnp.exp(jnp.sum(lq2_ref[...] * lk2_ref[...], axis=-1, keepdims=True)) + LAM_INIT)
        o = acc1 * (1.0 / l1) - acc2 * (lam / l2)
        ms = jnp.mean(o * o, axis=0, keepdims=True)
        y = o * lax.rsqrt(ms + EPS) * (subln_ref[...] * (1.0 - LAM_INIT))
        o_ref[block_rows(qb), :] = y.T.astype(BF16)

    @pl.when(bounded_ref[0] == 1)
    def _():
        def probs(j, sub, qb, bias_index):
            s = jnp.dot(key_tile(j), q_block(sub, qb), preferred_element_type=F32)
            p = jnp.exp2(s + bias_ref[bias_index])
            l_scr[sub, qb] += jnp.sum(p.reshape(t // SUBLANES, SUBLANES, t), axis=0)
            return p.astype(BF16)

        def weighted(j, sub, qb, p):
            acc_scr[sub, qb] += jnp.dot(value_tile(j), p, preferred_element_type=F32)

        def step(r, consume, produce):
            for u in range(ATT_UNROLL):
                for qb in range(Q_BLOCKS):
                    for sub in range(2):
                        if consume:
                            weighted(ATT_UNROLL * (r - 1) + u, sub, qb, p_scr[sub, u, qb])
                        if produce:
                            j = ATT_UNROLL * r + u
                            p_scr[sub, u, qb] = probs(j, sub, qb, jnp.clip(Q_BLOCKS * i + qb - j, -1, ND_ATT) + 1)

        def full_step(r, carry):
            step(r, True, True)
            return carry

        acc_scr[...] = jnp.zeros(acc_scr.shape, F32)
        l_scr[...] = jnp.zeros(l_scr.shape, F32)

        n_full = (Q_BLOCKS // ATT_UNROLL) * i

        @pl.when(i > 0)
        def _():
            step(0, False, True)
            lax.fori_loop(1, n_full, full_step, 0)
            step(n_full, True, False)

        for u in range(Q_BLOCKS):
            for qb in range(u, Q_BLOCKS):
                for sub in range(2):
                    j = Q_BLOCKS * i + u
                    weighted(j, sub, qb, probs(j, sub, qb, min(qb - u, ND_ATT) + 1))

        for qb in range(Q_BLOCKS):
            finish(qb, acc_scr[0, qb], jnp.sum(l_scr[0, qb], axis=0, keepdims=True),
                   acc_scr[1, qb], jnp.sum(l_scr[1, qb], axis=0, keepdims=True))

    @pl.when(bounded_ref[0] != 1)
    def _():
        def query_block(qb, carry):
            q_sub = (q_block(0, qb), q_block(1, qb))

            def body(j, state):
                kt = key_tile(j)
                vt = value_tile(j)
                b = bias_tile(j, qb)
                new = []
                for sub in range(2):
                    m, l, acc = state[sub]
                    s = jnp.dot(kt, q_sub[sub], preferred_element_type=F32) + b
                    m_new = jnp.maximum(m, jnp.max(s, axis=0, keepdims=True))
                    alpha = jnp.exp2(m - m_new)
                    p = jnp.exp2(s - m_new)
                    l_new = alpha * l + jnp.sum(p, axis=0, keepdims=True)
                    acc_new = alpha * acc + jnp.dot(vt, p.astype(BF16), preferred_element_type=F32)
                    new.append((m_new, l_new, acc_new))
                return tuple(new)

            init1 = (jnp.full((1, t), NEG_BIG, F32), jnp.zeros((1, t), F32), jnp.zeros((DA_V_DIM, t), F32))
            (_, l1, acc1), (_, l2, acc2) = lax.fori_loop(0, Q_BLOCKS * i + qb + 1, body, (init1, init1))
            finish(qb, acc1, l1, acc2, l2)
            return carry

        lax.fori_loop(0, Q_BLOCKS, query_block, 0)


def _diff_attn(bounded, lq1, lk1, lq2, lk2, outT, outN, bias, subln):
    s = outN.shape[1]
    t = T_ATT
    tq = Q_BLOCKS * t
    vec = pl.BlockSpec((1, DA_HEAD_DIM), lambda h, i: (0, 0))
    return pl.pallas_call(
        _attn_kernel,
        grid=(DA_HEADS, s // tq),
        in_specs=[
            pl.BlockSpec(memory_space=pltpu.SMEM),
            pl.BlockSpec((None, DA_V_DIM, tq), lambda h, i: (0, h, i)),
            pl.BlockSpec((None, s, DA_V_DIM), lambda h, i: (0, 0, h)),
            pl.BlockSpec((None, DA_V_DIM, s), lambda h, i: (1, h, 0)),
            pl.BlockSpec((N_BIAS_TILES, None, t, t), lambda h, i: (0, h, 0, 0)),
            pl.BlockSpec((DA_V_DIM, 1), lambda h, i: (0, 0)),
            vec, vec, vec, vec,
        ],
        out_specs=pl.BlockSpec((tq, DA_V_DIM), lambda h, i: (i, h)),
        out_shape=jax.ShapeDtypeStruct((s, ATTN_WIDTH), BF16),
        scratch_shapes=[
            pltpu.VMEM((2, ATT_UNROLL, Q_BLOCKS, t, t), BF16),
            pltpu.VMEM((2, Q_BLOCKS, DA_V_DIM, t), F32),
            pltpu.VMEM((2, Q_BLOCKS, SUBLANES, t), F32),
        ],
        compiler_params=pltpu.CompilerParams(
            dimension_semantics=("arbitrary", "arbitrary"), vmem_limit_bytes=VMEM_LIMIT),
        name="diff_attn",
    )(bounded, outT, outN, outT, bias, subln, lq1, lk1, lq2, lk2)


def _rglru_kernel(xr_ref, yr_ref, cw_ref, cb_ref, wg_ref, gab_ref, gxb_ref, lam_ref, gn_ref, o_ref,
                  xpad, hstate, a_scr, b_scr, h_scr):
    step = pl.program_id(0)
    tr = TR_RNN
    c = RNN_WIDTH

    @pl.when(step == 0)
    def _():
        xpad[0:SUBLANES, :] = jnp.zeros((SUBLANES, c), F32)
        hstate[...] = jnp.zeros((1, c), F32)

    x = xr_ref[...].astype(F32)
    xpad[SUBLANES:SUBLANES + tr, :] = x
    xc = cb_ref[...] + cw_ref[CONV_WIDTH - 1:CONV_WIDTH, :] * x
    for back in range(1, CONV_WIDTH):
        w_row = cw_ref[CONV_WIDTH - 1 - back:CONV_WIDTH - back, :]
        xc = xc + w_row * xpad[SUBLANES - back:SUBLANES - back + tr, :]
    xpad[0:SUBLANES, :] = x[tr - SUBLANES:tr, :]

    xcb = xc.astype(BF16)
    ga_parts, gx_parts = [], []
    for g in range(RNN_BLOCKS):
        gg = jnp.dot(xcb[:, g * RNN_BLOCK:(g + 1) * RNN_BLOCK], wg_ref[g], preferred_element_type=F32)
        ga_parts.append(gg[:, :RNN_BLOCK])
        gx_parts.append(gg[:, RNN_BLOCK:])
    gate_a = jax.nn.sigmoid(jnp.concatenate(ga_parts, axis=-1) + gab_ref[...])
    gate_x = jax.nn.sigmoid(jnp.concatenate(gx_parts, axis=-1) + gxb_ref[...])
    z = -lam_ref[...]
    softplus = jnp.maximum(z, 0.0) + jnp.log1p(jnp.exp(-jnp.abs(z)))
    log_a = -RG_C * gate_a * softplus
    a = jnp.exp(log_a)
    one_m_a2 = 1.0 - a * a
    root = jnp.where(one_m_a2 > 0.0, one_m_a2 * lax.rsqrt(one_m_a2), 0.0)
    u = (xc * gate_x) * root

    groups = tr // SUBLANES
    a3 = a.reshape(groups, SUBLANES, c)
    b3 = u.reshape(groups, SUBLANES, c)
    row = lax.broadcasted_iota(jnp.int32, a3.shape, 1)
    shift = 1
    while shift < SUBLANES:
        a_sh = pltpu.roll(a3, shift, axis=1)
        b_sh = pltpu.roll(b3, shift, axis=1)
        keep = row >= shift
        b3 = jnp.where(keep, a3 * b_sh + b3, b3)
        a3 = jnp.where(keep, a3 * a_sh, a3)
        shift *= 2
    a_scr[...] = a3.reshape(tr, c)
    b_scr[...] = b3.reshape(tr, c)

    def carry_body(g, h):
        r0 = pl.multiple_of(g * SUBLANES, SUBLANES)
        out = a_scr[pl.ds(r0, SUBLANES), :] * h + b_scr[pl.ds(r0, SUBLANES), :]
        h_scr[pl.ds(r0, SUBLANES), :] = out
        return out[SUBLANES - 1:SUBLANES, :]

    hstate[...] = lax.fori_loop(0, groups, carry_body, hstate[...])

    y = yr_ref[...].astype(F32)
    gelu = 0.5 * y * (1.0 + jnp.tanh(math.sqrt(2.0 / math.pi) * (y + 0.044715 * (y * y * y))))
    o_ref[...] = (_rms(h_scr[...] * gelu) * gn_ref[...]).astype(BF16)


def _rglru(outN, conv_w, conv_b, w_gates, ga_b, gx_b, rg_lambda, gn):
    s = outN.shape[1]
    c = RNN_WIDTH
    row = pl.BlockSpec((1, c), lambda t: (0, 0))
    return pl.pallas_call(
        _rglru_kernel,
        grid=(s // TR_RNN,),
        in_specs=[
            pl.BlockSpec((None, TR_RNN, c), lambda t: (1, t, 0)),
            pl.BlockSpec((None, TR_RNN, c), lambda t: (2, t, 0)),
            pl.BlockSpec((CONV_WIDTH, c), lambda t: (0, 0)),
            row,
            pl.BlockSpec((RNN_BLOCKS, RNN_BLOCK, 2 * RNN_BLOCK), lambda t: (0, 0, 0)),
            row, row, row, row,
        ],
        out_specs=pl.BlockSpec((TR_RNN, c), lambda t: (t, 0)),
        out_shape=jax.ShapeDtypeStruct((s, c), BF16),
        scratch_shapes=[
            pltpu.VMEM((TR_RNN + SUBLANES, c), F32),
            pltpu.VMEM((1, c), F32),
            pltpu.VMEM((TR_RNN, c), F32),
            pltpu.VMEM((TR_RNN, c), F32),
            pltpu.VMEM((TR_RNN, c), F32),
        ],
        compiler_params=pltpu.CompilerParams(
            dimension_semantics=("arbitrary",), vmem_limit_bytes=VMEM_LIMIT),
        name="rglru",
    )(outN, outN, conv_w, conv_b, w_gates, ga_b, gx_b, rg_lambda, gn)


def _outproj_kernel(x_ref, oa_ref, or_ref, w_ref, o_ref):
    y = jnp.dot(oa_ref[...], w_ref[0:ATTN_WIDTH, :], preferred_element_type=F32)
    y = y + jnp.dot(or_ref[...], w_ref[ATTN_WIDTH:, :], preferred_element_type=F32)
    o_ref[...] = x_ref[...] + y


def _out_proj(x, oa, orn, w):
    s = x.shape[0]
    tm = TM_PROJ
    return pl.pallas_call(
        _outproj_kernel,
        grid=(s // tm,),
        in_specs=[
            pl.BlockSpec((tm, D_MODEL), lambda i: (i, 0)),
            pl.BlockSpec((tm, ATTN_WIDTH), lambda i: (i, 0)),
            pl.BlockSpec((tm, RNN_WIDTH), lambda i: (i, 0)),
            pl.BlockSpec((ATTN_WIDTH + RNN_WIDTH, D_MODEL), lambda i: (0, 0)),
        ],
        out_specs=pl.BlockSpec((tm, D_MODEL), lambda i: (i, 0)),
        out_shape=jax.ShapeDtypeStruct((s, D_MODEL), F32),
        compiler_params=pltpu.CompilerParams(
            dimension_semantics=("arbitrary",), vmem_limit_bytes=VMEM_LIMIT),
        name="out_proj",
    )(x, oa, orn, w)


def _memkv_kernel(mem_ref, g_ref, wk_ref, wv_ref, kg_ref, kT_ref, v_ref):
    m = (_rms(mem_ref[...]) * g_ref[...]).astype(BF16)
    k = jnp.dot(m, wk_ref[...], preferred_element_type=F32)
    parts = [_rms(k[:, h * X_HEAD_DIM:(h + 1) * X_HEAD_DIM]) for h in range(X_HEADS)]
    kn = jnp.concatenate(parts, axis=-1) * kg_ref[...]
    kT_ref[...] = kn.T.astype(BF16)
    v_ref[...] = jnp.dot(m, wv_ref[...], preferred_element_type=F32).astype(BF16)


def _mem_kv(mem, g, wk, wv, kg):
    return pl.pallas_call(
        _memkv_kernel,
        out_shape=[
            jax.ShapeDtypeStruct((X_WIDTH, MEM_LEN), BF16),
            jax.ShapeDtypeStruct((MEM_LEN, X_WIDTH), BF16),
        ],
        compiler_params=pltpu.CompilerParams(vmem_limit_bytes=VMEM_LIMIT),
        name="mem_kv",
    )(mem, g, wk, wv, kg)


def _xattn_kernel(x_ref, wq_ref, kT_ref, v_ref, wo_ref, g_ref, qg_ref, o_ref):
    x = x_ref[...]
    hx = (_rms(x) * g_ref[...]).astype(BF16)
    q = jnp.dot(hx, wq_ref[...], preferred_element_type=F32)
    outs = []
    for h in range(X_HEADS):
        sl = slice(h * X_HEAD_DIM, (h + 1) * X_HEAD_DIM)
        qn = (_rms(q[:, sl]) * qg_ref[:, sl]).astype(BF16)
        s = jnp.dot(qn, kT_ref[sl, :], preferred_element_type=F32)
        p = jnp.exp2(s - jnp.max(s, axis=-1, keepdims=True))
        l = jnp.sum(p, axis=-1, keepdims=True)
        o = jnp.dot(p.astype(BF16), v_ref[:, sl], preferred_element_type=F32) / l
        outs.append(o.astype(BF16))
    o_all = jnp.concatenate(outs, axis=-1)
    o_ref[...] = x + jnp.dot(o_all, wo_ref[...], preferred_element_type=F32)


def _xattn(x, g, wq, qg, kT, v, wo):
    s = x.shape[0]
    tm = TM_PROJ
    full = lambda shape: pl.BlockSpec(shape, lambda i: tuple(0 for _ in shape))
    return pl.pallas_call(
        _xattn_kernel,
        grid=(s // tm,),
        in_specs=[
            pl.BlockSpec((tm, D_MODEL), lambda i: (i, 0)),
            full((D_MODEL, X_WIDTH)),
            full((X_WIDTH, MEM_LEN)),
            full((MEM_LEN, X_WIDTH)),
            full((X_WIDTH, D_MODEL)),
            full((1, D_MODEL)),
            full((1, X_WIDTH)),
        ],
        out_specs=pl.BlockSpec((tm, D_MODEL), lambda i: (i, 0)),
        out_shape=jax.ShapeDtypeStruct((s, D_MODEL), F32),
        compiler_params=pltpu.CompilerParams(
            dimension_semantics=("arbitrary",), vmem_limit_bytes=VMEM_LIMIT),
        name="xattn",
    )(x, wq, kT, v, wo, g, qg)


def _ffn_kernel(x_ref, wg_ref, wu_ref, wd_ref, g_ref, o_ref, h_scr):
    f = pl.program_id(1)

    @pl.when(f == 0)
    def _():
        x = x_ref[...]
        h_scr[...] = (_rms(x) * g_ref[...]).astype(BF16)
        o_ref[...] = x

    h = h_scr[...]
    gate = jnp.dot(h, wg_ref[...], preferred_element_type=F32)
    up = jnp.dot(h, wu_ref[...], preferred_element_type=F32)
    act = (gate * jax.nn.sigmoid(gate) * up).astype(BF16)
    o_ref[...] += jnp.dot(act, wd_ref[...], preferred_element_type=F32)


def _ffn(x, g, w_gate_up, w_down):
    s = x.shape[0]
    tm = TM_FFN
    nf = D_FF // TF_FFN
    return pl.pallas_call(
        _ffn_kernel,
        grid=(s // tm, nf),
        in_specs=[
            pl.BlockSpec((tm, D_MODEL), lambda i, f: (i, 0)),
            pl.BlockSpec((D_MODEL, TF_FFN), lambda i, f: (0, f)),
            pl.BlockSpec((D_MODEL, TF_FFN), lambda i, f: (0, nf + f)),
            pl.BlockSpec((TF_FFN, D_MODEL), lambda i, f: (f, 0)),
            pl.BlockSpec((1, D_MODEL), lambda i, f: (0, 0)),
        ],
        out_specs=pl.BlockSpec((tm, D_MODEL), lambda i, f: (i, 0)),
        out_shape=jax.ShapeDtypeStruct((s, D_MODEL), F32),
        scratch_shapes=[pltpu.VMEM((tm, D_MODEL), BF16)],
        compiler_params=pltpu.CompilerParams(
            dimension_semantics=("arbitrary", "arbitrary"), vmem_limit_bytes=VMEM_LIMIT),
        name="ffn",
    )(x, w_gate_up, w_gate_up, w_down, g)


def kernel(x, mem, rel_bias, attn_norm, w_in, da_q_norm, da_k_norm, da_lambda_q1, da_lambda_k1, da_lambda_q2, da_lambda_k2, da_subln, conv_w, conv_b, gate_a_w, gate_a_b, gate_x_w, gate_x_b, rg_lambda, rnn_out_norm, w_out, xattn_norm, mem_norm, xq_w, xk_w, xv_w, xq_norm, xk_norm, xo_w, ffn_norm, w_gate_up, w_down):
    assert x.shape == (1, SEQ, D_MODEL) and mem.shape == (1, MEM_LEN, D_MODEL)
    assert w_in.shape == (1, D_MODEL, 3 * ATTN_WIDTH + 2 * RNN_WIDTH)
    assert w_gate_up.shape == (1, D_MODEL, 2 * D_FF) and w_down.shape == (1, D_FF, D_MODEL)
    assert SEQ % (T_ATT * Q_BLOCKS) == 0 and SEQ % TM_PROJ == 0 and SEQ % TM_IN == 0 and SEQ % TM_FFN == 0
    assert SEQ % TR_RNN == 0 and D_FF % TF_FFN == 0
    x2d = x[0]
    row = lambda v: v.reshape(1, -1).astype(F32)

    a = ATTN_WIDTH
    wi = w_in[0]
    w_in_b = jnp.concatenate([wi[:, 0:a], wi[:, 2 * a:3 * a], wi[:, a:2 * a], wi[:, 3 * a:]], axis=1).astype(BF16)
    grp = jnp.arange(MXU_DIM) // DA_HEAD_DIM
    gsum = jnp.where(grp[:, None] == grp[None, :], 1.0 / DA_HEAD_DIM, 0.0).astype(BF16)
    reps = ATTN_WIDTH // DA_HEAD_DIM
    qg = row(jnp.tile(da_q_norm[0], reps)) * (DA_HEAD_DIM ** -0.5 * LOG2E)
    kg = row(jnp.tile(da_k_norm[0], reps))

    outT, outN = _in_proj(x2d, row(attn_norm[0]), w_in_b, gsum, qg, kg)

    shift, bounded = _logit_bound(rel_bias, da_q_norm[0], da_k_norm[0])
    bias = _bias_tiles(rel_bias.astype(F32), shift)
    o_attn = _diff_attn(bounded, row(da_lambda_q1[0]), row(da_lambda_k1[0]), row(da_lambda_q2[0]), row(da_lambda_k2[0]),
                        outT, outN, bias, da_subln[0].reshape(DA_V_DIM, 1).astype(F32))

    w_gates = jnp.concatenate([gate_a_w[0], gate_x_w[0]], axis=-1).astype(BF16)
    o_rnn = _rglru(outN, conv_w[0].astype(F32), row(conv_b[0]), w_gates, row(gate_a_b[0]), row(gate_x_b[0]),
                   row(rg_lambda[0]), row(rnn_out_norm[0]))

    x1 = _out_proj(x2d, o_attn, o_rnn, w_out[0].astype(BF16))

    kT, vm = _mem_kv(mem[0], row(mem_norm[0]), xk_w[0].astype(BF16), xv_w[0].astype(BF16),
                     row(jnp.tile(xk_norm[0], X_HEADS)))
    xqg = row(jnp.tile(xq_norm[0], X_HEADS)) * (X_HEAD_DIM ** -0.5 * LOG2E)
    x2 = _xattn(x1, row(xattn_norm[0]), xq_w[0].astype(BF16), xqg, kT, vm, xo_w[0].astype(BF16))

    x3 = _ffn(x2, row(ffn_norm[0]), w_gate_up[0].astype(BF16), w_down[0].astype(BF16))
    return x3[None]
```

```python
import functools
import math

import jax
import jax.numpy as jnp
from jax import lax
from jax.experimental import pallas as pl
from jax.experimental.pallas import tpu as pltpu

F32 = jnp.float32
BF16 = jnp.bfloat16

D_MODEL = 2048
SEQ = 16384
CHUNK = 64
CHUNK_SHIFT = CHUNK.bit_length() - 1
assert 1 << CHUNK_SHIFT == CHUNK
MEM_LEN = 256
EPS = 1e-6
ATTN_WIDTH = 1024
DA_HEAD_DIM = 64
DA_HEADS = 8
DA_V_DIM = 2 * DA_HEAD_DIM
RNN_WIDTH = 1024
RNN_BLOCKS = 8
RNN_BLOCK = RNN_WIDTH // RNN_BLOCKS
CONV_WIDTH = 4
RG_C = 8.0
X_HEADS = 4
X_HEAD_DIM = 128
X_WIDTH = X_HEADS * X_HEAD_DIM
D_FF = 5632
NUM_BUCKETS = 32
MAX_DISTANCE = 1024
LAM_INIT = 0.8 - 0.6 * math.exp(-0.3 * 0)
LOG2E = 1.4426950408889634
NEG_BIG = -1e30
MAX_EXP2_SPREAD = 120.0

LANES = 128
SUBLANES = 8
MXU_DIM = 256
VMEM_LIMIT = 56 * 1024 * 1024

TM_PROJ = 512
TM_IN = 512
TM_FFN = 1024
SEG = 1024
N_SEG = 5
T_ATT = 512
TR_RNN = 256
TF_FFN = 512


def _far_bucket_distance():
    half = NUM_BUCKETS // 2
    max_exact = half // 2
    n = max_exact
    while max_exact + int(math.log(n / max_exact) / math.log(MAX_DISTANCE / max_exact) * (half - max_exact)) < half - 1:
        n += 1
    return n


FAR_DIST = _far_bucket_distance()
ND_ATT = next(d for d in range(1, 64) if (d - 1) * T_ATT + 1 >= FAR_DIST + 64)
FAR_BUCKET = NUM_BUCKETS // 2 - 1
N_BIAS_TILES = ND_ATT + 2
ATT_UNROLL = 4
Q_BLOCKS = 4
assert Q_BLOCKS % ATT_UNROLL == 0


def _rms(x, eps=EPS):
    return x * lax.rsqrt(jnp.mean(x * x, axis=-1, keepdims=True) + eps)


def _inproj_kernel(x_ref, w_ref, gsum_ref, g_ref, qg_ref, kg_ref, outT_ref, outN_ref):
    h = (_rms(x_ref[...]) * g_ref[...]).astype(BF16)

    def group_norm(a, gain_row):
        sq = (a * a).astype(BF16)
        parts = [jnp.dot(sq[:, c * MXU_DIM:(c + 1) * MXU_DIM], gsum_ref[...], preferred_element_type=F32)
                 for c in range(SEG // MXU_DIM)]
        ms = jnp.concatenate(parts, axis=-1)
        return a * lax.rsqrt(ms + EPS) * gain_row

    for seg in range(N_SEG):
        acc = jnp.dot(h, w_ref[:, seg * SEG:(seg + 1) * SEG], preferred_element_type=F32)
        if seg == 0:
            outT_ref[0] = group_norm(acc, qg_ref[...]).T.astype(BF16)
        elif seg == 1:
            outT_ref[1] = acc.T.astype(BF16)
        elif seg == 2:
            outN_ref[0] = group_norm(acc, kg_ref[...]).astype(BF16)
        else:
            outN_ref[seg - 2] = acc.astype(BF16)


def _in_proj(x, g, w, gsum, qg, kg):
    s = x.shape[0]
    assert w.shape[1] == N_SEG * SEG
    return pl.pallas_call(
        _inproj_kernel,
        grid=(s // TM_IN,),
        in_specs=[
            pl.BlockSpec((TM_IN, D_MODEL), lambda i: (i, 0)),
            pl.BlockSpec((D_MODEL, N_SEG * SEG), lambda i: (0, 0), pipeline_mode=pl.Buffered(1)),
            pl.BlockSpec((MXU_DIM, MXU_DIM), lambda i: (0, 0)),
            pl.BlockSpec((1, D_MODEL), lambda i: (0, 0)),
            pl.BlockSpec((1, SEG), lambda i: (0, 0)),
            pl.BlockSpec((1, SEG), lambda i: (0, 0)),
        ],
        out_specs=[
            pl.BlockSpec((2, SEG, TM_IN), lambda i: (0, 0, i)),
            pl.BlockSpec((N_SEG - 2, TM_IN, SEG), lambda i: (0, i, 0)),
        ],
        out_shape=[
            jax.ShapeDtypeStruct((2, SEG, s), BF16),
            jax.ShapeDtypeStruct((N_SEG - 2, s, SEG), BF16),
        ],
        compiler_params=pltpu.CompilerParams(
            dimension_semantics=("arbitrary",), vmem_limit_bytes=VMEM_LIMIT),
        name="in_proj",
    )(x, w, gsum, g, qg, kg)


def _bias_kernel(rb_ref, shift_ref, out_ref):
    d = pl.program_id(0) - 1
    t = T_ATT
    width = 2 * t
    half = NUM_BUCKETS // 2

    @pl.when(jnp.logical_and(d >= 0, d < ND_ATT))
    def _():
        c = lax.broadcasted_iota(jnp.int32, (SUBLANES, width), 1)
        rel = jnp.where(c < t, -c, width - c) - d * t
        max_exact = half // 2
        n = jnp.abs(rel)
        large = max_exact + (jnp.log(jnp.maximum(n, 1).astype(F32) / max_exact)
                             / math.log(MAX_DISTANCE / max_exact) * (half - max_exact)).astype(jnp.int32)
        large = jnp.minimum(large, half - 1)
        bucket = jnp.where(rel > 0, half, 0) + jnp.where(n < max_exact, n, large)
        kk = lax.broadcasted_iota(jnp.int32, (t, t), 0)
        qq = lax.broadcasted_iota(jnp.int32, (t, t), 1)
        allowed = (kk >> CHUNK_SHIFT) <= ((qq + d * t) >> CHUNK_SHIFT)

        def head(h, carry):
            vec = jnp.zeros((SUBLANES, width), F32)
            for b in range(NUM_BUCKETS):
                vec = jnp.where(bucket == b, rb_ref[b, h], vec)
            vec = (vec - rb_ref[FAR_BUCKET, h]) * LOG2E - shift_ref[h]
            rows = jnp.broadcast_to(vec[0:1, :], (t, width))
            tile = pltpu.roll(rows, 0, 1, stride=1, stride_axis=0)[:, :t]
            out_ref[h] = jnp.where(allowed, tile, NEG_BIG)
            return carry
        lax.fori_loop(0, DA_HEADS, head, 0)

    @pl.when(jnp.logical_or(d < 0, d >= ND_ATT))
    def _():
        def head(h, carry):
            out_ref[h] = jnp.full((t, t), jnp.where(d < 0, NEG_BIG, -shift_ref[h]), F32)
            return carry
        lax.fori_loop(0, DA_HEADS, head, 0)


def _bias_tiles(rel_bias, shift):
    return pl.pallas_call(
        _bias_kernel,
        grid=(N_BIAS_TILES,),
        in_specs=[pl.BlockSpec(memory_space=pltpu.SMEM), pl.BlockSpec(memory_space=pltpu.SMEM)],
        out_specs=pl.BlockSpec((None, DA_HEADS, T_ATT, T_ATT), lambda d: (d, 0, 0, 0)),
        out_shape=jax.ShapeDtypeStruct((N_BIAS_TILES, DA_HEADS, T_ATT, T_ATT), F32),
        compiler_params=pltpu.CompilerParams(
            dimension_semantics=("arbitrary",), vmem_limit_bytes=VMEM_LIMIT),
        name="bias_tiles",
    )(rel_bias, shift)


def _logit_bound(rel_bias, q_gain, k_gain):
    rb = rel_bias.astype(F32)
    b2 = (rb - rb[FAR_BUCKET]) * LOG2E
    qk = (LOG2E * DA_HEAD_DIM ** 0.5 * (1.0 + 2.0 ** -6)) * jnp.max(jnp.abs(q_gain)) * jnp.max(jnp.abs(k_gain))
    bound = qk + jnp.max(b2, axis=0)
    spread = 2.0 * qk + jnp.max(jnp.max(b2, axis=0) - jnp.min(b2, axis=0))
    bounded = spread <= MAX_EXP2_SPREAD
    return jnp.where(bounded, bound, 0.0).astype(F32), bounded.astype(jnp.int32).reshape(1)


def _attn_kernel(bounded_ref, qT_ref, k_ref, vT_ref, bias_ref, subln_ref, lq1_ref, lk1_ref, lq2_ref, lk2_ref,
                 o_ref, p_scr, acc_scr, l_scr):
    i = pl.program_id(1)
    t = T_ATT

    def block_rows(qb):
        return pl.ds(qb * t if isinstance(qb, int) else pl.multiple_of(qb * t, t), t)

    def q_block(sub, qb):
        q = qT_ref[:, block_rows(qb)]
        row = lax.broadcasted_iota(jnp.int32, q.shape, 0)
        keep = row < DA_HEAD_DIM if sub == 0 else row >= DA_HEAD_DIM
        return jnp.where(keep, q, jnp.zeros_like(q))

    def key_tile(j):
        return k_ref[pl.ds(pl.multiple_of(j * t, t), t), :]

    def value_tile(j):
        return vT_ref[:, pl.ds(pl.multiple_of(j * t, t), t)]

    def bias_tile(j, qb):
        return bias_ref[jnp.clip(Q_BLOCKS * i + qb - j, -1, ND_ATT) + 1]

    def finish(qb, acc1, l1, acc2, l2):
        lam = (jnp.exp(jnp.sum(lq1_ref[...] * lk1_ref[...], axis=-1, keepdims=True))
               - jnp.exp(jnp.sum(lq2_ref[...] * lk2_ref[...], axis=-1, keepdims=True)) + LAM_INIT)
        o = acc1 * (1.0 / l1) - acc2 * (lam / l2)
        ms = jnp.mean(o * o, axis=0, keepdims=True)
        y = o * lax.rsqrt(ms + EPS) * (subln_ref[...] * (1.0 - LAM_INIT))
        o_ref[block_rows(qb), :] = y.T.astype(BF16)

    @pl.when(bounded_ref[0] == 1)
    def _():
        def probs(j, sub, qb, bias_index):
            s = jnp.dot(key_tile(j), q_block(sub, qb), preferred_element_type=F32)
            p = jnp.exp2(s + bias_ref[bias_index])
            l_scr[sub, qb] += jnp.sum(p.reshape(t // SUBLANES, SUBLANES, t), axis=0)
            return p.astype(BF16)

        def weighted(j, sub, qb, p):
            acc_scr[sub, qb] += jnp.dot(value_tile(j), p, preferred_element_type=F32)

        def step(r, consume, produce):
            for u in range(ATT_UNROLL):
                for qb in range(Q_BLOCKS):
                    for sub in range(2):
                        if consume:
                            weighted(ATT_UNROLL * (r - 1) + u, sub, qb, p_scr[sub, u, qb])
                        if produce:
                            j = ATT_UNROLL * r + u
                            p_scr[sub, u, qb] = probs(j, sub, qb, jnp.clip(Q_BLOCKS * i + qb - j, -1, ND_ATT) + 1)

        def full_step(r, carry):
            step(r, True, True)
            return carry

        acc_scr[...] = jnp.zeros(acc_scr.shape, F32)
        l_scr[...] = jnp.zeros(l_scr.shape, F32)

        n_full = (Q_BLOCKS // ATT_UNROLL) * i

        @pl.when(i > 0)
        def _():
            step(0, False, True)
            lax.fori_loop(1, n_full, full_step, 0)
            step(n_full, True, False)

        for u in range(Q_BLOCKS):
            for qb in range(u, Q_BLOCKS):
                for sub in range(2):
                    j = Q_BLOCKS * i + u
                    weighted(j, sub, qb, probs(j, sub, qb, min(qb - u, ND_ATT) + 1))

        for qb in range(Q_BLOCKS):
            finish(qb, acc_scr[0, qb], jnp.sum(l_scr[0, qb], axis=0, keepdims=True),
                   acc_scr[1, qb], jnp.sum(l_scr[1, qb], axis=0, keepdims=True))

    @pl.when(bounded_ref[0] != 1)
    def _():
        def query_block(qb, carry):
            q_sub = (q_block(0, qb), q_block(1, qb))

            def body(j, state):
                kt = key_tile(j)
                vt = value_tile(j)
                b = bias_tile(j, qb)
                new = []
                for sub in range(2):
                    m, l, acc = state[sub]
                    s = jnp.dot(kt, q_sub[sub], preferred_element_type=F32) + b
                    m_new = jnp.maximum(m, jnp.max(s, axis=0, keepdims=True))
                    alpha = jnp.exp2(m - m_new)
                    p = jnp.exp2(s - m_new)
                    l_new = alpha * l + jnp.sum(p, axis=0, keepdims=True)
                    acc_new = alpha * acc + jnp.dot(vt, p.astype(BF16), preferred_element_type=F32)
                    new.append((m_new, l_new, acc_new))
                return tuple(new)

            init1 = (jnp.full((1, t), NEG_BIG, F32), jnp.zeros((1, t), F32), jnp.zeros((DA_V_DIM, t), F32))
            (_, l1, acc1), (_, l2, acc2) = lax.fori_loop(0, Q_BLOCKS * i + qb + 1, body, (init1, init1))
            finish(qb, acc1, l1, acc2, l2)
            return carry

        lax.fori_loop(0, Q_BLOCKS, query_block, 0)


def _diff_attn(bounded, lq1, lk1, lq2, lk2, outT, outN, bias, subln):
    s = outN.shape[1]
    t = T_ATT
    tq = Q_BLOCKS * t
    vec = pl.BlockSpec((1, DA_HEAD_DIM), lambda h, i: (0, 0))
    return pl.pallas_call(
        _attn_kernel,
        grid=(DA_HEADS, s // tq),
        in_specs=[
            pl.BlockSpec(memory_space=pltpu.SMEM),
            pl.BlockSpec((None, DA_V_DIM, tq), lambda h, i: (0, h, i)),
            pl.BlockSpec((None, s, DA_V_DIM), lambda h, i: (0, 0, h)),
            pl.BlockSpec((None, DA_V_DIM, s), lambda h, i: (1, h, 0)),
            pl.BlockSpec((N_BIAS_TILES, None, t, t), lambda h, i: (0, h, 0, 0)),
            pl.BlockSpec((DA_V_DIM, 1), lambda h, i: (0, 0)),
            vec, vec, vec, vec,
        ],
        out_specs=pl.BlockSpec((tq, DA_V_DIM), lambda h, i: (i, h)),
        out_shape=jax.ShapeDtypeStruct((s, ATTN_WIDTH), BF16),
        scratch_shapes=[
            pltpu.VMEM((2, ATT_UNROLL, Q_BLOCKS, t, t), BF16),
            pltpu.VMEM((2, Q_BLOCKS, DA_V_DIM, t), F32),
            pltpu.VMEM((2, Q_BLOCKS, SUBLANES, t), F32),
        ],
        compiler_params=pltpu.CompilerParams(
            dimension_semantics=("arbitrary", "arbitrary"), vmem_limit_bytes=VMEM_LIMIT),
        name="diff_attn",
    )(bounded, outT, outN, outT, bias, subln, lq1, lk1, lq2, lk2)


def _rglru_rows(x, y, cw_ref, cb_ref, wg_ref, gab_ref, gxb_ref, lam_ref, gn_ref, xpad, hstate):
    tr = TR_RNN
    c = RNN_WIDTH
    xpad[SUBLANES:SUBLANES + tr, :] = x
    xc = cb_ref[...] + cw_ref[CONV_WIDTH - 1:CONV_WIDTH, :] * x
    for back in range(1, CONV_WIDTH):
        w_row = cw_ref[CONV_WIDTH - 1 - back:CONV_WIDTH - back, :]
        xc = xc + w_row * xpad[SUBLANES - back:SUBLANES - back + tr, :]
    xpad[0:SUBLANES, :] = x[tr - SUBLANES:tr, :]

    xcb = xc.astype(BF16)
    ga_parts, gx_parts = [], []
    for g in range(RNN_BLOCKS):
        gg = jnp.dot(xcb[:, g * RNN_BLOCK:(g + 1) * RNN_BLOCK], wg_ref[g], preferred_element_type=F32)
        ga_parts.append(gg[:, :RNN_BLOCK])
        gx_parts.append(gg[:, RNN_BLOCK:])
    gate_a = jax.nn.sigmoid(jnp.concatenate(ga_parts, axis=-1) + gab_ref[...])
    gate_x = jax.nn.sigmoid(jnp.concatenate(gx_parts, axis=-1) + gxb_ref[...])
    z = -lam_ref[...]
    softplus = jnp.maximum(z, 0.0) + jnp.log1p(jnp.exp(-jnp.abs(z)))
    log_a = -RG_C * gate_a * softplus
    a = jnp.exp(log_a)
    one_m_a2 = 1.0 - a * a
    root = jnp.where(one_m_a2 > 0.0, one_m_a2 * lax.rsqrt(one_m_a2), 0.0)
    u = (xc * gate_x) * root

    groups = tr // SUBLANES
    a3 = a.reshape(groups, SUBLANES, c)
    b3 = u.reshape(groups, SUBLANES, c)
    row = lax.broadcasted_iota(jnp.int32, a3.shape, 1)
    shift = 1
    while shift < SUBLANES:
        a_sh = pltpu.roll(a3, shift, axis=1)
        b_sh = pltpu.roll(b3, shift, axis=1)
        keep = row >= shift
        b3 = jnp.where(keep, a3 * b_sh + b3, b3)
        a3 = jnp.where(keep, a3 * a_sh, a3)
        shift *= 2

    h = hstate[...]
    outs = []
    for g in range(groups):
        out = a3[g] * h + b3[g]
        outs.append(out)
        h = out[SUBLANES - 1:SUBLANES, :]
    hstate[...] = h
    h_all = jnp.concatenate(outs, axis=0)

    gelu = 0.5 * y * (1.0 + jnp.tanh(math.sqrt(2.0 / math.pi) * (y + 0.044715 * (y * y * y))))
    return (_rms(h_all * gelu) * gn_ref[...]).astype(BF16)


def _rnn_out_kernel(x_ref, oa_ref, xr_ref, yr_ref, cw_ref, cb_ref, wg_ref, gab_ref, gxb_ref, lam_ref, gn_ref, w_ref,
                    o_ref, xpad, hstate, ornn_scr):
    @pl.when(pl.program_id(0) == 0)
    def _():
        xpad[0:SUBLANES, :] = jnp.zeros((SUBLANES, RNN_WIDTH), F32)
        hstate[...] = jnp.zeros(hstate.shape, F32)
        ornn_scr[...] = jnp.zeros(ornn_scr.shape, BF16)

    for part in range(TM_PROJ // TR_RNN):
        rows = slice(part * TR_RNN, (part + 1) * TR_RNN)
        y = jnp.dot(oa_ref[rows, :], w_ref[0:ATTN_WIDTH, :], preferred_element_type=F32)
        y = y + jnp.dot(ornn_scr[rows, :], w_ref[ATTN_WIDTH:, :], preferred_element_type=F32)
        o_ref[rows, :] = x_ref[rows, :] + y
        ornn_scr[rows, :] = _rglru_rows(xr_ref[rows, :].astype(F32), yr_ref[rows, :].astype(F32), cw_ref, cb_ref,
                                        wg_ref, gab_ref, gxb_ref, lam_ref, gn_ref, xpad, hstate)


def _rnn_out(x, oa, outN, conv_w, conv_b, w_gates, ga_b, gx_b, rg_lambda, gn, w_out):
    s = x.shape[0]
    c = RNN_WIDTH
    tm = TM_PROJ
    n = s // tm
    row = pl.BlockSpec((1, c), lambda t: (0, 0))
    prev = lambda t: (jnp.maximum(t - 1, 0), 0)
    return pl.pallas_call(
        _rnn_out_kernel,
        grid=(n + 1,),
        in_specs=[
            pl.BlockSpec((tm, D_MODEL), prev),
            pl.BlockSpec((tm, ATTN_WIDTH), prev),
            pl.BlockSpec((None, tm, c), lambda t: (1, jnp.minimum(t, n - 1), 0)),
            pl.BlockSpec((None, tm, c), lambda t: (2, jnp.minimum(t, n - 1), 0)),
            pl.BlockSpec((CONV_WIDTH, c), lambda t: (0, 0)),
            row,
            pl.BlockSpec((RNN_BLOCKS, RNN_BLOCK, 2 * RNN_BLOCK), lambda t: (0, 0, 0)),
            row, row, row, row,
            pl.BlockSpec((ATTN_WIDTH + RNN_WIDTH, D_MODEL), lambda t: (0, 0)),
        ],
        out_specs=pl.BlockSpec((tm, D_MODEL), prev),
        out_shape=jax.ShapeDtypeStruct((s, D_MODEL), F32),
        scratch_shapes=[
            pltpu.VMEM((TR_RNN + SUBLANES, c), F32),
            pltpu.VMEM((1, c), F32),
            pltpu.VMEM((tm, c), BF16),
        ],
        compiler_params=pltpu.CompilerParams(
            dimension_semantics=("arbitrary",), vmem_limit_bytes=VMEM_LIMIT),
        name="rnn_out",
    )(x, oa, outN, outN, conv_w, conv_b, w_gates, ga_b, gx_b, rg_lambda, gn, w_out)


def _memkv_kernel(mem_ref, g_ref, wk_ref, wv_ref, kg_ref, kT_ref, v_ref):
    m = (_rms(mem_ref[...]) * g_ref[...]).astype(BF16)
    k = jnp.dot(m, wk_ref[...], preferred_element_type=F32)
    parts = [_rms(k[:, h * X_HEAD_DIM:(h + 1) * X_HEAD_DIM]) for h in range(X_HEADS)]
    kn = jnp.concatenate(parts, axis=-1) * kg_ref[...]
    kT_ref[...] = kn.T.astype(BF16)
    v_ref[...] = jnp.dot(m, wv_ref[...], preferred_element_type=F32).astype(BF16)


def _mem_kv(mem, g, wk, wv, kg):
    return pl.pallas_call(
        _memkv_kernel,
        out_shape=[
            jax.ShapeDtypeStruct((X_WIDTH, MEM_LEN), BF16),
            jax.ShapeDtypeStruct((MEM_LEN, X_WIDTH), BF16),
        ],
        compiler_params=pltpu.CompilerParams(vmem_limit_bytes=VMEM_LIMIT),
        name="mem_kv",
    )(mem, g, wk, wv, kg)


def _xattn_kernel(x_ref, wq_ref, kT_ref, v_ref, wo_ref, g_ref, qg_ref, o_ref):
    x = x_ref[...]
    hx = (_rms(x) * g_ref[...]).astype(BF16)
    q = jnp.dot(hx, wq_ref[...], preferred_element_type=F32)
    outs = []
    for h in range(X_HEADS):
        sl = slice(h * X_HEAD_DIM, (h + 1) * X_HEAD_DIM)
        qn = (_rms(q[:, sl]) * qg_ref[:, sl]).astype(BF16)
        s = jnp.dot(qn, kT_ref[sl, :], preferred_element_type=F32)
        p = jnp.exp2(s - jnp.max(s, axis=-1, keepdims=True))
        l = jnp.sum(p, axis=-1, keepdims=True)
        o = jnp.dot(p.astype(BF16), v_ref[:, sl], preferred_element_type=F32) / l
        outs.append(o.astype(BF16))
    o_all = jnp.concatenate(outs, axis=-1)
    o_ref[...] = x + jnp.dot(o_all, wo_ref[...], preferred_element_type=F32)


def _xattn(x, g, wq, qg, kT, v, wo):
    s = x.shape[0]
    tm = TM_PROJ
    full = lambda shape: pl.BlockSpec(shape, lambda i: tuple(0 for _ in shape))
    return pl.pallas_call(
        _xattn_kernel,
        grid=(s // tm,),
        in_specs=[
            pl.BlockSpec((tm, D_MODEL), lambda i: (i, 0)),
            full((D_MODEL, X_WIDTH)),
            full((X_WIDTH, MEM_LEN)),
            full((MEM_LEN, X_WIDTH)),
            full((X_WIDTH, D_MODEL)),
            full((1, D_MODEL)),
            full((1, X_WIDTH)),
        ],
        out_specs=pl.BlockSpec((tm, D_MODEL), lambda i: (i, 0)),
        out_shape=jax.ShapeDtypeStruct((s, D_MODEL), F32),
        compiler_params=pltpu.CompilerParams(
            dimension_semantics=("arbitrary",), vmem_limit_bytes=VMEM_LIMIT),
        name="xattn",
    )(x, wq, kT, v, wo, g, qg)


def _ffn_kernel(x_ref, wg_ref, wu_ref, wd_ref, g_ref, o_ref, h_scr):
    f = pl.program_id(1)

    @pl.when(f == 0)
    def _():
        x = x_ref[...]
        h_scr[...] = (_rms(x) * g_ref[...]).astype(BF16)
        o_ref[...] = x

    h = h_scr[...]
    gate = jnp.dot(h, wg_ref[...], preferred_element_type=F32)
    up = jnp.dot(h, wu_ref[...], preferred_element_type=F32)
    act = (gate * jax.nn.sigmoid(gate) * up).astype(BF16)
    o_ref[...] += jnp.dot(act, wd_ref[...], preferred_element_type=F32)


def _ffn(x, g, w_gate_up, w_down):
    s = x.shape[0]
    tm = TM_FFN
    nf = D_FF // TF_FFN
    return pl.pallas_call(
        _ffn_kernel,
        grid=(s // tm, nf),
        in_specs=[
            pl.BlockSpec((tm, D_MODEL), lambda i, f: (i, 0)),
            pl.BlockSpec((D_MODEL, TF_FFN), lambda i, f: (0, f)),
            pl.BlockSpec((D_MODEL, TF_FFN), lambda i, f: (0, nf + f)),
            pl.BlockSpec((TF_FFN, D_MODEL), lambda i, f: (f, 0)),
            pl.BlockSpec((1, D_MODEL), lambda i, f: (0, 0)),
        ],
        out_specs=pl.BlockSpec((tm, D_MODEL), lambda i, f: (i, 0)),
        out_shape=jax.ShapeDtypeStruct((s, D_MODEL), F32),
        scratch_shapes=[pltpu.VMEM((tm, D_MODEL), BF16)],
        compiler_params=pltpu.CompilerParams(
            dimension_semantics=("arbitrary", "arbitrary"), vmem_limit_bytes=VMEM_LIMIT),
        name="ffn",
    )(x, w_gate_up, w_gate_up, w_down, g)


def kernel(x, mem, rel_bias, attn_norm, w_in, da_q_norm, da_k_norm, da_lambda_q1, da_lambda_k1, da_lambda_q2, da_lambda_k2, da_subln, conv_w, conv_b, gate_a_w, gate_a_b, gate_x_w, gate_x_b, rg_lambda, rnn_out_norm, w_out, xattn_norm, mem_norm, xq_w, xk_w, xv_w, xq_norm, xk_norm, xo_w, ffn_norm, w_gate_up, w_down):
    assert x.shape == (1, SEQ, D_MODEL) and mem.shape == (1, MEM_LEN, D_MODEL)
    assert w_in.shape == (1, D_MODEL, 3 * ATTN_WIDTH + 2 * RNN_WIDTH)
    assert w_gate_up.shape == (1, D_MODEL, 2 * D_FF) and w_down.shape == (1, D_FF, D_MODEL)
    assert SEQ % (T_ATT * Q_BLOCKS) == 0 and SEQ % TM_PROJ == 0 and SEQ % TM_IN == 0 and SEQ % TM_FFN == 0
    assert SEQ % TR_RNN == 0 and D_FF % TF_FFN == 0
    x2d = x[0]
    row = lambda v: v.reshape(1, -1).astype(F32)

    a = ATTN_WIDTH
    wi = w_in[0]
    w_in_b = jnp.concatenate([wi[:, 0:a], wi[:, 2 * a:3 * a], wi[:, a:2 * a], wi[:, 3 * a:]], axis=1).astype(BF16)
    grp = jnp.arange(MXU_DIM) // DA_HEAD_DIM
    gsum = jnp.where(grp[:, None] == grp[None, :], 1.0 / DA_HEAD_DIM, 0.0).astype(BF16)
    reps = ATTN_WIDTH // DA_HEAD_DIM
    qg = row(jnp.tile(da_q_norm[0], reps)) * (DA_HEAD_DIM ** -0.5 * LOG2E)
    kg = row(jnp.tile(da_k_norm[0], reps))

    outT, outN = _in_proj(x2d, row(attn_norm[0]), w_in_b, gsum, qg, kg)

    shift, bounded = _logit_bound(rel_bias, da_q_norm[0], da_k_norm[0])
    bias = _bias_tiles(rel_bias.astype(F32), shift)
    o_attn = _diff_attn(bounded, row(da_lambda_q1[0]), row(da_lambda_k1[0]), row(da_lambda_q2[0]), row(da_lambda_k2[0]),
                        outT, outN, bias, da_subln[0].reshape(DA_V_DIM, 1).astype(F32))

    w_gates = jnp.concatenate([gate_a_w[0], gate_x_w[0]], axis=-1).astype(BF16)
    x1 = _rnn_out(x2d, o_attn, outN, conv_w[0].astype(F32), row(conv_b[0]), w_gates, row(gate_a_b[0]),
                  row(gate_x_b[0]), row(rg_lambda[0]), row(rnn_out_norm[0]), w_out[0].astype(BF16))

    kT, vm = _mem_kv(mem[0], row(mem_norm[0]), xk_w[0].astype(BF16), xv_w[0].astype(BF16),
                     row(jnp.tile(xk_norm[0], X_HEADS)))
    xqg = row(jnp.tile(xq_norm[0], X_HEADS)) * (X_HEAD_DIM ** -0.5 * LOG2E)
    x2 = _xattn(x1, row(xattn_norm[0]), xq_w[0].astype(BF16), xqg, kT, vm, xo_w[0].astype(BF16))

    x3 = _ffn(x2, row(ffn_norm[0]), w_gate_up[0].astype(BF16), w_down[0].astype(BF16))
    return x3[None]
```
